```python
import jax, jax.numpy as jnp
from jax import lax
import numpy as np

D_MODEL = 2048
BATCH = 1
SEQ = 8192
DEPTH = 1

SB_HEADS = 16
SB_HEAD_DIM = 128
SB_BLOCK = 128
GDN_HEADS = 16
GDN_K_DIM = 128
GDN_V_DIM = 128
GDN_CONV = 4
GDN_CHUNK = 64
D_FF = -(-8 * D_MODEL // (3 * 256)) * 256
EPS = 1e-6

D_SB = SB_HEADS * SB_HEAD_DIM
D_GDN_K = GDN_HEADS * GDN_K_DIM
D_GDN_V = GDN_HEADS * GDN_V_DIM
D_GDN_QKV = 2 * D_GDN_K + D_GDN_V
IN_SPLITS = (D_SB, D_SB, D_SB, D_GDN_QKV, D_GDN_V, GDN_HEADS, GDN_HEADS, D_MODEL, D_MODEL)
IN_OFFSETS = tuple(int(o) for o in np.cumsum(IN_SPLITS)[:-1])
D_IN = int(sum(IN_SPLITS))

kernel_name = 'hybrid_stickbreak_gdn_adaln_block'


def _rms(x, w):
    xf = x.astype(jnp.float32)
    return xf * lax.rsqrt(jnp.mean(xf * xf, axis=-1, keepdims=True) + EPS) * w.astype(jnp.float32)


def _l2norm(x):
    return x * lax.rsqrt(jnp.sum(x * x, axis=-1, keepdims=True) + EPS)


def _heads(t, n, d):
    b, s, _ = t.shape
    return t.reshape(b, s, n, d).transpose(0, 2, 1, 3)


def _merge_heads(t):
    b, n, s, d = t.shape
    return t.transpose(0, 2, 1, 3).reshape(b, s, n * d)


def _causal_depthwise_conv(x, w):
    return lax.conv_general_dilated(
        x, w[:, None, :].astype(x.dtype), window_strides=(1,), padding=[(GDN_CONV - 1, 0)],
        dimension_numbers=('NWC', 'WIO', 'NWC'), feature_group_count=x.shape[-1])


def _stick_breaking(q, k, v):
    s_len, d = q.shape[2], q.shape[3]
    scale = d ** -0.5
    outs = []
    for i in range(s_len // SB_BLOCK):
        q0 = i * SB_BLOCK
        kend = q0 + SB_BLOCK
        z = jnp.einsum('bhqd,bhkd->bhqk', q[:, :, q0:kend], k[:, :, :kend]) * scale
        causal = jnp.arange(kend)[None, :] < (q0 + jnp.arange(SB_BLOCK))[:, None]
        log_1m = jnp.where(causal, jax.nn.log_sigmoid(-z), 0.0)
        after = lax.cumsum(log_1m, axis=3, reverse=True) - log_1m
        w = jnp.where(causal, jnp.exp(jax.nn.log_sigmoid(z) + after), 0.0)
        outs.append(jnp.einsum('bhqk,bhkd->bhqd', w, v[:, :, :kend]))
    return jnp.concatenate(outs, axis=2)


def _gated_delta_rule(q, k, v, g, beta):
    b, h, s_len, dk = q.shape
    dv = v.shape[-1]
    n, c = s_len // GDN_CHUNK, GDN_CHUNK
    q = q.reshape(b, h, n, c, dk)
    k = k.reshape(b, h, n, c, dk)
    v = v.reshape(b, h, n, c, dv)
    g = g.reshape(b, h, n, c)
    beta = beta.reshape(b, h, n, c)
    gc = jnp.cumsum(g, axis=-1)
    tril = jnp.tril(jnp.ones((c, c), dtype=bool))
    stril = jnp.tril(jnp.ones((c, c), dtype=bool), k=-1)
    diff = gc[..., :, None] - gc[..., None, :]
    decay = jnp.where(tril, jnp.exp(jnp.where(tril, diff, 0.0)), 0.0)
    kk = jnp.einsum('bhnrd,bhnid->bhnri', k, k)
    m = jnp.eye(c, dtype=q.dtype) + jnp.where(stril, beta[..., :, None] * kk * decay, 0.0)
    solve = lambda rhs: lax.linalg.triangular_solve(m, rhs, left_side=True, lower=True, unit_diagonal=True)
    w_v = solve(beta[..., None] * v)
    w_k = solve((beta * jnp.exp(gc))[..., None] * k)
    attn = jnp.einsum('bhnrd,bhnid->bhnri', q, k) * decay
    q_g = q * jnp.exp(gc)[..., None]
    k_dec = k * jnp.exp(gc[..., -1:] - gc)[..., None]
    g_last = jnp.exp(gc[..., -1])

    def step(state, xs):
        w_v_c, w_k_c, q_g_c, attn_c, k_dec_c, g_last_c = xs
        u = w_v_c - jnp.einsum('bhcd,bhde->bhce', w_k_c, state)
        o = jnp.einsum('bhcd,bhde->bhce', q_g_c, state) + jnp.einsum('bhcj,bhje->bhce', attn_c, u)
        state = g_last_c[..., None, None] * state + jnp.einsum('bhcd,bhce->bhde', k_dec_c, u)
        return state, o

    mv = lambda t: jnp.moveaxis(t, 2, 0)
    state0 = jnp.zeros((b, h, dk, dv), dtype=q.dtype)
    _, o = lax.scan(step, state0, (mv(w_v), mv(w_k), mv(q_g), mv(attn), mv(k_dec), mv(g_last)))
    return jnp.moveaxis(o, 0, 2).reshape(b, h, s_len, dv)


def setup_inputs(seed: int = 0) -> dict:
    key = jax.random.key(seed)
    ks = jax.random.split(key, 20)
    f32 = jnp.float32
    nrm = lambda k, shape, s: jax.random.normal(k, shape, f32) * s
    gain = lambda k, shape: 1.0 + 0.02 * jax.random.normal(k, shape, f32)
    dt = jnp.exp(jax.random.uniform(ks[10], (DEPTH, GDN_HEADS), f32, np.log(1e-3), np.log(1e-1)))
    return {
        'x': nrm(ks[0], (BATCH, SEQ, D_MODEL), 1.0),
        'c': nrm(ks[1], (BATCH, D_MODEL), 1.0),
        'w_mod': nrm(ks[2], (DEPTH, D_MODEL, 6 * D_MODEL), 0.5 * D_MODEL ** -0.5),
        'b_mod': nrm(ks[3], (DEPTH, 6 * D_MODEL), 0.02),
        'norm1_w': gain(ks[4], (DEPTH, D_MODEL)),
        'w_in': nrm(ks[5], (DEPTH, D_MODEL, D_IN), D_MODEL ** -0.5),
        'q_norm_w': gain(ks[6], (DEPTH, SB_HEAD_DIM)),
        'k_norm_w': gain(ks[7], (DEPTH, SB_HEAD_DIM)),
        'conv_w': nrm(ks[8], (DEPTH, GDN_CONV, D_GDN_QKV), GDN_CONV ** -0.5),
        'a_log': jnp.log(jax.random.uniform(ks[9], (DEPTH, GDN_HEADS), f32, 1.0, 16.0)),
        'dt_bias': dt + jnp.log(-jnp.expm1(-dt)),
        'o_norm_w': gain(ks[11], (DEPTH, GDN_V_DIM)),
        'p_a': nrm(ks[12], (DEPTH, D_SB, D_MODEL), D_SB ** -0.5),
        'p_b': nrm(ks[13], (DEPTH, D_GDN_V, D_MODEL), D_GDN_V ** -0.5),
        'w_out': nrm(ks[14], (DEPTH, D_MODEL, D_MODEL), D_MODEL ** -0.5),
        'norm2_w': gain(ks[15], (DEPTH, D_MODEL)),
        'w_gate': nrm(ks[16], (DEPTH, D_MODEL, D_FF), D_MODEL ** -0.5),
        'w_up': nrm(ks[17], (DEPTH, D_MODEL, D_FF), D_MODEL ** -0.5),
        'w_down': nrm(ks[18], (DEPTH, D_FF, D_MODEL), D_FF ** -0.5),
    }


def reference(x, c, w_mod, b_mod, norm1_w, w_in, q_norm_w, k_norm_w, conv_w, a_log, dt_bias,
              o_norm_w, p_a, p_b, w_out, norm2_w, w_gate, w_up, w_down):
    f32 = jnp.float32
    bsz, s_len, _ = x.shape
    h = x.astype(f32)
    c_act = jax.nn.silu(c.astype(f32))
    for l in range(DEPTH):
        mod = c_act @ w_mod[l].astype(f32) + b_mod[l].astype(f32)
        shift1, scale1, gate1, shift2, scale2, gate2 = [t[:, None, :] for t in jnp.split(mod, 6, axis=-1)]

        u = _rms(h, norm1_w[l]) * (1.0 + scale1) + shift1
        proj = u @ w_in[l].astype(f32)
        qa, ka, va, qkv_b, z_b, b_b, a_b, gate_a, gate_b = jnp.split(proj, IN_OFFSETS, axis=-1)

        qa = _rms(_heads(qa, SB_HEADS, SB_HEAD_DIM), q_norm_w[l])
        ka = _rms(_heads(ka, SB_HEADS, SB_HEAD_DIM), k_norm_w[l])
        va = _heads(va, SB_HEADS, SB_HEAD_DIM)
        o_a = _merge_heads(_stick_breaking(qa, ka, va))

        qkv_b = jax.nn.silu(_causal_depthwise_conv(qkv_b, conv_w[l]))
        qb, kb, vb = jnp.split(qkv_b, (D_GDN_K, 2 * D_GDN_K), axis=-1)
        qb = _l2norm(_heads(qb, GDN_HEADS, GDN_K_DIM)) * (GDN_K_DIM ** -0.5)
        kb = _l2norm(_heads(kb, GDN_HEADS, GDN_K_DIM))
        vb = _heads(vb, GDN_HEADS, GDN_V_DIM)
        beta = jax.nn.sigmoid(b_b).transpose(0, 2, 1)
        g = (-jnp.exp(a_log[l].astype(f32)) * jax.nn.softplus(a_b + dt_bias[l].astype(f32))).transpose(0, 2, 1)
        o_b = _gated_delta_rule(qb, kb, vb, g, beta)
        o_b = _rms(o_b, o_norm_w[l]) * jax.nn.silu(_heads(z_b, GDN_HEADS, GDN_V_DIM))
        o_b = _merge_heads(o_b)

        merged = (jax.nn.sigmoid(gate_a) * (o_a @ p_a[l].astype(f32))
                  + jax.nn.sigmoid(gate_b) * (o_b @ p_b[l].astype(f32)))
        h = h + gate1 * (merged @ w_out[l].astype(f32))

        u = _rms(h, norm2_w[l]) * (1.0 + scale2) + shift2
        ff = jax.nn.silu(u @ w_gate[l].astype(f32)) * (u @ w_up[l].astype(f32))
        h = h + gate2 * (ff @ w_down[l].astype(f32))
    return h.astype(x.dtype)
```

```python
import functools

import jax
import jax.numpy as jnp
from jax import lax
from jax.experimental import pallas as pl
from jax.experimental.pallas import tpu as pltpu

F32 = jnp.float32
BF16 = jnp.bfloat16

EPS = 1e-6
HEAD_DIM = 128
N_HEADS = 16
GDN_CONV = 4
LANES = 128
SUBLANES = 8
VMEM_LIMIT_BYTES = 56 * 1024 * 1024

SB_TILE = 256
GDN_CHUNK = 128
GDN_HEADS_PER_STEP = 4


def _params(*sem):
    return pltpu.CompilerParams(dimension_semantics=sem, vmem_limit_bytes=VMEM_LIMIT_BYTES)


def _sigmoid(x):
    return 1.0 / (1.0 + jnp.exp(-x))


def _silu(x):
    return x * _sigmoid(x)


def _softplus(x):
    return jnp.maximum(x, 0.0) + jnp.log1p(jnp.exp(-jnp.abs(x)))


def _bdot(a, b):
    return jnp.dot(a.astype(BF16), b.astype(BF16), preferred_element_type=F32)


def _bdot_nt(a, b):
    return lax.dot_general(a.astype(BF16), b.astype(BF16), (((1,), (1,)), ((), ())),
                           preferred_element_type=F32)


def _adaln_body(c_ref, w_ref, b_ref, o_ref):
    c = c_ref[...]
    o_ref[...] = jnp.sum(w_ref[...] * _silu(c), axis=0, keepdims=True) + b_ref[...]


def _adaln(c_col, w, b, *, tn=1024):
    d, n = w.shape
    return pl.pallas_call(
        _adaln_body,
        grid=(n // tn,),
        in_specs=[pl.BlockSpec((d, 1), lambda j: (0, 0)),
                  pl.BlockSpec((d, tn), lambda j: (0, j)),
                  pl.BlockSpec((1, tn), lambda j: (0, j))],
        out_specs=pl.BlockSpec((1, tn), lambda j: (0, j)),
        out_shape=jax.ShapeDtypeStruct((1, n), F32),
        compiler_params=_params("parallel"),
        name="adaln_mod",
    )(c_col, w, b)


def _norm_mod_body(x_ref, w_ref, scale_ref, shift_ref, o_ref):
    x = x_ref[...]
    y = x * lax.rsqrt(jnp.mean(x * x, axis=-1, keepdims=True) + EPS) * w_ref[...]
    o_ref[...] = (y * (1.0 + scale_ref[...]) + shift_ref[...]).astype(o_ref.dtype)


def _norm_mod(x, w, scale, shift, *, tm=512):
    s, d = x.shape
    row = pl.BlockSpec((1, d), lambda i: (0, 0))
    return pl.pallas_call(
        _norm_mod_body,
        grid=(s // tm,),
        in_specs=[pl.BlockSpec((tm, d), lambda i: (i, 0)), row, row, row],
        out_specs=pl.BlockSpec((tm, d), lambda i: (i, 0)),
        out_shape=jax.ShapeDtypeStruct((s, d), BF16),
        compiler_params=_params("parallel"),
        name="norm_mod",
    )(x, w, scale, shift)


def _mm_body(*refs, n_pairs, n_extra, epilogue):
    a_refs = refs[:n_pairs]
    b_refs = refs[n_pairs:2 * n_pairs]
    extra = refs[2 * n_pairs:2 * n_pairs + n_extra]
    o_ref = refs[2 * n_pairs + n_extra]
    accs = [jnp.dot(a[...], b[...], preferred_element_type=F32) for a, b in zip(a_refs, b_refs)]
    o_ref[...] = epilogue(accs, [e[...] for e in extra]).astype(o_ref.dtype)


def _matmul(pairs, extras, epilogue, n, out_dtype, *, tm, tn, name):
    m = pairs[0][0].shape[0]
    in_specs, args = [], []
    for a, _, _ in pairs:
        in_specs.append(pl.BlockSpec((tm, a.shape[1]), lambda i, j: (i, 0)))
        args.append(a)
    for _, b, off in pairs:
        in_specs.append(pl.BlockSpec((b.shape[0], tn), lambda i, j, off=off: (0, j + off)))
        args.append(b)
    for arr, kind, off in extras:
        if kind == "tile":
            in_specs.append(pl.BlockSpec((tm, tn), lambda i, j, off=off: (i, j + off)))
        else:
            in_specs.append(pl.BlockSpec((1, tn), lambda i, j: (0, j)))
        args.append(arr)
    body = functools.partial(_mm_body, n_pairs=len(pairs), n_extra=len(extras), epilogue=epilogue)
    return pl.pallas_call(
        body,
        grid=(m // tm, n // tn),
        in_specs=in_specs,
        out_specs=pl.BlockSpec((tm, tn), lambda i, j: (i, j)),
        out_shape=jax.ShapeDtypeStruct((m, n), out_dtype),
        compiler_params=_params("parallel", "parallel"),
        name=name,
    )(*args)


def _ep_plain(accs, extra):
    return accs[0]


def _ep_head_rmsnorm(accs, extra):
    acc, w = accs[0], extra[0]
    outs = []
    for g in range(acc.shape[1] // HEAD_DIM):
        blk = acc[:, g * HEAD_DIM:(g + 1) * HEAD_DIM]
        ms = jnp.mean(blk * blk, axis=-1, keepdims=True)
        outs.append(blk * lax.rsqrt(ms + EPS) * w[:, g * HEAD_DIM:(g + 1) * HEAD_DIM])
    return jnp.concatenate(outs, axis=1)


def _ep_sigmoid(accs, extra):
    return _sigmoid(accs[0])


def _ep_merge(accs, extra):
    return extra[0] * accs[0] + extra[1] * accs[1]


def _ep_residual(accs, extra):
    return extra[0] + extra[1] * accs[0]


def _ep_swiglu(accs, extra):
    return _silu(accs[0]) * accs[1]


def _sb_body(q_ref, k_ref, v_ref, o_ref, acc_ref, carry_ref, *, t):
    i = pl.program_id(1)
    q = q_ref[...]
    row = lax.broadcasted_iota(jnp.int32, (t, t), 0)
    col = lax.broadcasted_iota(jnp.int32, (t, t), 1)
    suffix = (row > col).astype(BF16)
    causal = col < row

    def tile(j, masked):
        start = pl.multiple_of(j * t, t)
        kj = k_ref[pl.ds(start, t), :]
        vj = v_ref[pl.ds(start, t), :]
        z = lax.dot_general(q, kj, (((1,), (1,)), ((), ())), preferred_element_type=F32)
        sp = jnp.log1p(jnp.exp(-jnp.abs(z)))
        ls = jnp.minimum(z, 0.0) - sp
        l1m = ls - z
        if masked:
            l1m = jnp.where(causal, l1m, 0.0)
        hi = l1m.astype(BF16)
        lo = (l1m - hi.astype(F32)).astype(BF16)
        within = (jnp.dot(hi, suffix, preferred_element_type=F32)
                  + jnp.dot(lo, suffix, preferred_element_type=F32))
        carry = carry_ref[...]
        after = within + jnp.concatenate([carry] * (t // LANES), axis=1)
        w = jnp.exp(ls + after)
        if masked:
            w = jnp.where(causal, w, 0.0)
        acc_ref[...] += jnp.dot(w.astype(BF16), vj, preferred_element_type=F32)
        carry_ref[...] = carry + jnp.sum(l1m, axis=1, keepdims=True)

    acc_ref[...] = jnp.zeros_like(acc_ref)
    carry_ref[...] = jnp.zeros_like(carry_ref)
    tile(i, True)

    def body(n, c):
        tile(i - 1 - n, False)
        return c

    lax.fori_loop(0, i, body, 0)
    o_ref[...] = acc_ref[...].astype(o_ref.dtype)


def _sb_attention(qk, v, *, t=SB_TILE):
    s = v.shape[0]
    return pl.pallas_call(
        functools.partial(_sb_body, t=t),
        grid=(N_HEADS, s // t),
        in_specs=[pl.BlockSpec((t, HEAD_DIM), lambda h, i: (i, h)),
                  pl.BlockSpec((s, HEAD_DIM), lambda h, i: (0, N_HEADS + h)),
                  pl.BlockSpec((s, HEAD_DIM), lambda h, i: (0, h))],
        out_specs=pl.BlockSpec((t, HEAD_DIM), lambda h, i: (i, h)),
        out_shape=jax.ShapeDtypeStruct((s, N_HEADS * HEAD_DIM), BF16),
        scratch_shapes=[pltpu.VMEM((t, HEAD_DIM), F32), pltpu.VMEM((t, LANES), F32)],
        compiler_params=_params("parallel", "arbitrary"),
        name="stick_breaking",
    )(qk, qk, v)


def _unit_lower_inverse(strict_lower, row, col):
    c = strict_lower.shape[0]
    inv = (row == col).astype(F32)
    lvl = 0
    while (1 << lvl) < c:
        in_pair = ((row >> (lvl + 1)) == (col >> (lvl + 1))) & ((row >> lvl) != (col >> lvl))
        off = jnp.where(in_pair, strict_lower, 0.0)
        if lvl == 0:
            inv = inv - off
        else:
            inv = inv - _bdot(_bdot(inv, off), inv)
        lvl += 1
    return inv


def _gdn_body(q_ref, k_ref, v_ref, ba_ref, z_ref, cw_ref, alog_ref, dtb_ref, onw_ref, o_ref,
              state_ref, tail_ref, gct_ref, *, c, hb):
    grp = pl.program_id(0)
    step = pl.program_id(1)

    @pl.when(step == 0)
    def _():
        state_ref[...] = jnp.zeros_like(state_ref)
        tail_ref[...] = jnp.zeros_like(tail_ref)

    row = lax.broadcasted_iota(jnp.int32, (c, c), 0)
    col = lax.broadcasted_iota(jnp.int32, (c, c), 1)
    lane = lax.broadcasted_iota(jnp.int32, (1, LANES), 1)

    ba = ba_ref[...]
    beta_all = _sigmoid(ba)
    g_all = -jnp.exp(alog_ref[...]) * _softplus(ba + dtb_ref[...])
    incl = (row >= col).astype(F32)
    gc_all = jnp.dot(incl, g_all, precision=lax.Precision.HIGHEST,
                     preferred_element_type=F32)
    gct_ref[...] = gc_all.T

    def conv_silu(x_ref, sec, s):
        cur = x_ref[:, s * HEAD_DIM:(s + 1) * HEAD_DIM]
        lanes = slice((sec * hb + s) * HEAD_DIM, (sec * hb + s + 1) * HEAD_DIM)
        ext = jnp.concatenate([tail_ref[:, lanes], cur], axis=0)
        w = cw_ref[:, lanes]
        y = cur * w[GDN_CONV - 1:GDN_CONV, :]
        for j in range(GDN_CONV - 1):
            shifted = pltpu.roll(ext, GDN_CONV - 1 - j, axis=0)[SUBLANES:, :]
            y = y + shifted * w[j:j + 1, :]
        tail_ref[:, lanes] = cur[c - SUBLANES:, :]
        return _silu(y)

    def l2norm(x):
        return x * lax.rsqrt(jnp.sum(x * x, axis=-1, keepdims=True) + EPS)

    def column(x, idx):
        picked = jnp.sum(jnp.where(lane == idx, x, 0.0), axis=1, keepdims=True)
        return jnp.broadcast_to(picked, (c, LANES))

    for s in range(hb):
        h = grp * hb + s
        q = l2norm(conv_silu(q_ref, 0, s)) * (HEAD_DIM ** -0.5)
        k = l2norm(conv_silu(k_ref, 1, s))
        v = conv_silu(v_ref, 2, s)

        beta = column(beta_all, h)
        gc = column(gc_all, N_HEADS + h)
        gc_row = gct_ref[pl.ds(N_HEADS + h, 1), :]
        gc_last = gc[c - 1:c, :]
        exp_gc = jnp.exp(gc)

        lower = row >= col
        decay = jnp.where(lower, jnp.exp(jnp.where(lower, gc - gc_row, 0.0)), 0.0)
        kk = _bdot_nt(k, k)
        attn = _bdot_nt(q, k) * decay
        strict = jnp.where(row > col, beta * kk * decay, 0.0)
        tinv = _unit_lower_inverse(strict, row, col)
        rhs = jnp.concatenate([beta * v, beta * exp_gc * k], axis=1)
        w_vk = _bdot(tinv, rhs)
        w_v, w_k = w_vk[:, :HEAD_DIM], w_vk[:, HEAD_DIM:]

        state = state_ref[s]
        u = w_v - _bdot(w_k, state)
        o = _bdot(q * exp_gc, state) + _bdot(attn, u)
        k_dec = k * jnp.exp(gc_last - gc)
        state_ref[s] = jnp.exp(gc_last) * state + _bdot(k_dec.T, u)

        ms = jnp.mean(o * o, axis=-1, keepdims=True)
        zh = z_ref[:, s * HEAD_DIM:(s + 1) * HEAD_DIM]
        o_ref[:, s * HEAD_DIM:(s + 1) * HEAD_DIM] = (
            o * lax.rsqrt(ms + EPS) * onw_ref[...] * _silu(zh)).astype(o_ref.dtype)


def _gdn(qkv, ba, z, conv_w, alog_row, dtb_row, onw_row, *, c=GDN_CHUNK, hb=GDN_HEADS_PER_STEP):
    s = qkv.shape[0]
    groups = N_HEADS // hb
    wide = hb * HEAD_DIM
    sec = lambda k: pl.BlockSpec((c, wide), lambda g, t, k=k: (t, k * groups + g))
    cw_sec = lambda k: (0, k)
    cw = conv_w.reshape(GDN_CONV, 3, groups, wide).transpose(2, 0, 1, 3).reshape(groups, GDN_CONV, 3 * wide)
    row = pl.BlockSpec((1, LANES), lambda g, t: (0, 0))
    return pl.pallas_call(
        functools.partial(_gdn_body, c=c, hb=hb),
        grid=(groups, s // c),
        in_specs=[sec(0), sec(1), sec(2),
                  pl.BlockSpec((c, LANES), lambda g, t: (t, 0)),
                  pl.BlockSpec((c, wide), lambda g, t: (t, g)),
                  pl.BlockSpec((None, GDN_CONV, 3 * wide), lambda g, t: (g, 0, 0)),
                  row, row, row],
        out_specs=pl.BlockSpec((c, wide), lambda g, t: (t, g)),
        out_shape=jax.ShapeDtypeStruct((s, N_HEADS * HEAD_DIM), BF16),
        scratch_shapes=[pltpu.VMEM((hb, HEAD_DIM, HEAD_DIM), F32),
                        pltpu.VMEM((SUBLANES, 3 * wide), F32),
                        pltpu.VMEM((LANES, c), F32)],
        compiler_params=_params("parallel", "arbitrary"),
        name="gated_deltanet",
    )(qkv, qkv, qkv, ba, z, cw, alog_row, dtb_row, onw_row)


def kernel(x, c, w_mod, b_mod, norm1_w, w_in, q_norm_w, k_norm_w, conv_w, a_log, dt_bias,
           o_norm_w, p_a, p_b, w_out, norm2_w, w_gate, w_up, w_down):
    bsz, s, d = x.shape
    depth = w_mod.shape[0]
    assert bsz == 1, "single-sequence prefill block"
    hd = N_HEADS * HEAD_DIM
    d_ff = w_gate.shape[2]
    h = x[0].astype(F32)
    c_col = c.astype(F32).reshape(d, 1)

    o_v, o_qkvb, o_z = 2 * hd, 3 * hd, 6 * hd
    o_b, o_gates = 7 * hd, 7 * hd + 2 * N_HEADS

    for l in range(depth):
        mod = _adaln(c_col, w_mod[l].astype(F32), b_mod[l].astype(F32).reshape(1, 6 * d))
        shift1, scale1, gate1, shift2, scale2, gate2 = [mod[:, i * d:(i + 1) * d] for i in range(6)]

        w_main = w_in[l][:, :o_b].astype(BF16)
        w_ba = jnp.pad(w_in[l][:, o_b:o_gates], ((0, 0), (0, LANES - 2 * N_HEADS))).astype(BF16)
        w_gates = w_in[l][:, o_gates:].astype(BF16)

        u = _norm_mod(h, norm1_w[l].astype(F32).reshape(1, d), scale1, shift1)

        tm, tn = 512, 1024
        qk_norm_w = jnp.concatenate([jnp.tile(q_norm_w[l].astype(F32) * (HEAD_DIM ** -0.5), N_HEADS),
                                     jnp.tile(k_norm_w[l].astype(F32), N_HEADS)]).reshape(1, 2 * hd)
        qk_a = _matmul([(u, w_main, 0)], [(qk_norm_w, "row", 0)], _ep_head_rmsnorm, 2 * hd, BF16,
                       tm=tm, tn=tn, name="proj_qk")
        v_a = _matmul([(u, w_main, o_v // tn)], [], _ep_plain, hd, BF16, tm=tm, tn=tn, name="proj_v")
        qkv_b = _matmul([(u, w_main, o_qkvb // tn)], [], _ep_plain, 3 * hd, F32, tm=tm, tn=tn,
                        name="proj_qkv_b")
        z_b = _matmul([(u, w_main, o_z // tn)], [], _ep_plain, hd, F32, tm=tm, tn=tn, name="proj_z")
        ba = _matmul([(u, w_ba, 0)], [], _ep_plain, LANES, F32, tm=tm, tn=LANES, name="proj_ba")
        sig_gates = _matmul([(u, w_gates, 0)], [], _ep_sigmoid, 2 * d, BF16, tm=tm, tn=tn,
                            name="proj_gates")

        o_a = _sb_attention(qk_a, v_a)

        pad = (0, LANES - 2 * N_HEADS)
        alog_row = jnp.pad(jnp.concatenate([jnp.zeros((N_HEADS,), F32), a_log[l].astype(F32)]), pad)
        dtb_row = jnp.pad(jnp.concatenate([jnp.zeros((N_HEADS,), F32), dt_bias[l].astype(F32)]), pad)
        o_b_out = _gdn(qkv_b, ba, z_b, conv_w[l].astype(F32), alog_row.reshape(1, LANES),
                       dtb_row.reshape(1, LANES), o_norm_w[l].astype(F32).reshape(1, HEAD_DIM))

        merged = _matmul([(o_a, p_a[l].astype(BF16), 0), (o_b_out, p_b[l].astype(BF16), 0)],
                         [(sig_gates, "tile", 0), (sig_gates, "tile", d // tn)], _ep_merge, d, BF16,
                         tm=tm, tn=tn, name="merge_proj")
        h = _matmul([(merged, w_out[l].astype(BF16), 0)], [(h, "tile", 0), (gate1, "row", 0)],
                    _ep_residual, d, F32, tm=tm, tn=tn, name="out_proj")

        u2 = _norm_mod(h, norm2_w[l].astype(F32).reshape(1, d), scale2, shift2)
        tf = 512
        ff = _matmul([(u2, w_gate[l].astype(BF16), 0), (u2, w_up[l].astype(BF16), 0)], [],
                     _ep_swiglu, d_ff, BF16, tm=tm, tn=tf, name="ffn_up")
        h = _matmul([(ff, w_down[l].astype(BF16), 0)], [(h, "tile", 0), (gate2, "row", 0)],
                    _ep_residual, d, F32, tm=tm, tn=tn, name="ffn_down")
    return h.reshape(bsz, s, d).astype(x.dtype)
```

```python
import functools

import jax
import jax.numpy as jnp
from jax import lax
from jax.experimental import pallas as pl
from jax.experimental.pallas import tpu as pltpu

F32 = jnp.float32
BF16 = jnp.bfloat16

EPS = 1e-6
EXP_UNDERFLOW = 105.0
HEAD_DIM = 128
N_HEADS = 16
GDN_CONV = 4
LANES = 128
SUBLANES = 8
VMEM_LIMIT_BYTES = 56 * 1024 * 1024

SB_K_TILE = 256
SB_Q_TILE = 256
SB_HEADS_PER_STEP = 4
GDN_CHUNK = 128
GDN_HEADS_PER_STEP = 16


def _params(*sem):
    return pltpu.CompilerParams(dimension_semantics=sem, vmem_limit_bytes=VMEM_LIMIT_BYTES)


def _sigmoid(x):
    return 1.0 / (1.0 + jnp.exp(-x))


def _silu(x):
    return x * _sigmoid(x)


def _softplus(x):
    return jnp.maximum(x, 0.0) + jnp.log1p(jnp.exp(-jnp.abs(x)))


def _bdot(a, b):
    return jnp.dot(a.astype(BF16), b.astype(BF16), preferred_element_type=F32)


def _bdot_nt(a, b):
    return lax.dot_general(a.astype(BF16), b.astype(BF16), (((1,), (1,)), ((), ())),
                           preferred_element_type=F32)


def _adaln_body(c_ref, w_ref, b_ref, o_ref):
    c = c_ref[...]
    o_ref[...] = jnp.sum(w_ref[...] * _silu(c), axis=0, keepdims=True) + b_ref[...]


def _adaln(c_col, w, b, *, tn=1024):
    d, n = w.shape
    return pl.pallas_call(
        _adaln_body,
        grid=(n // tn,),
        in_specs=[pl.BlockSpec((d, 1), lambda j: (0, 0)),
                  pl.BlockSpec((d, tn), lambda j: (0, j)),
                  pl.BlockSpec((1, tn), lambda j: (0, j))],
        out_specs=pl.BlockSpec((1, tn), lambda j: (0, j)),
        out_shape=jax.ShapeDtypeStruct((1, n), F32),
        compiler_params=_params("parallel"),
        name="adaln_mod",
    )(c_col, w, b)


def _norm_mod_body(x_ref, w_ref, scale_ref, shift_ref, o_ref):
    x = x_ref[...]
    y = x * lax.rsqrt(jnp.mean(x * x, axis=-1, keepdims=True) + EPS) * w_ref[...]
    o_ref[...] = (y * (1.0 + scale_ref[...]) + shift_ref[...]).astype(o_ref.dtype)


def _norm_mod(x, w, scale, shift, *, tm=512):
    s, d = x.shape
    row = pl.BlockSpec((1, d), lambda i: (0, 0))
    return pl.pallas_call(
        _norm_mod_body,
        grid=(s // tm,),
        in_specs=[pl.BlockSpec((tm, d), lambda i: (i, 0)), row, row, row],
        out_specs=pl.BlockSpec((tm, d), lambda i: (i, 0)),
        out_shape=jax.ShapeDtypeStruct((s, d), BF16),
        compiler_params=_params("parallel"),
        name="norm_mod",
    )(x, w, scale, shift)


def _mm_body(*refs, n_pairs, n_extra, epilogue):
    a_refs = refs[:n_pairs]
    b_refs = refs[n_pairs:2 * n_pairs]
    extra = refs[2 * n_pairs:2 * n_pairs + n_extra]
    o_ref = refs[2 * n_pairs + n_extra]
    accs = [jnp.dot(a[...], b[...], preferred_element_type=F32) for a, b in zip(a_refs, b_refs)]
    o_ref[...] = epilogue(accs, [e[...] for e in extra]).astype(o_ref.dtype)


def _matmul(pairs, extras, epilogue, n, out_dtype, *, tm, tn, name):
    m = pairs[0][0].shape[0]
    in_specs, args = [], []
    for a, _, _ in pairs:
        in_specs.append(pl.BlockSpec((tm, a.shape[1]), lambda i, j: (i, 0)))
        args.append(a)
    for _, b, off in pairs:
        in_specs.append(pl.BlockSpec((b.shape[0], tn), lambda i, j, off=off: (0, j + off)))
        args.append(b)
    for arr, kind, off in extras:
        if kind == "tile":
            in_specs.append(pl.BlockSpec((tm, tn), lambda i, j, off=off: (i, j + off)))
        else:
            in_specs.append(pl.BlockSpec((1, tn), lambda i, j: (0, j)))
        args.append(arr)
    body = functools.partial(_mm_body, n_pairs=len(pairs), n_extra=len(extras), epilogue=epilogue)
    return pl.pallas_call(
        body,
        grid=(m // tm, n // tn),
        in_specs=in_specs,
        out_specs=pl.BlockSpec((tm, tn), lambda i, j: (i, j)),
        out_shape=jax.ShapeDtypeStruct((m, n), out_dtype),
        compiler_params=_params("parallel", "parallel"),
        name=name,
    )(*args)


def _ep_plain(accs, extra):
    return accs[0]


def _ep_head_rmsnorm(accs, extra):
    acc, w = accs[0], extra[0]
    outs = []
    for g in range(acc.shape[1] // HEAD_DIM):
        blk = acc[:, g * HEAD_DIM:(g + 1) * HEAD_DIM]
        ms = jnp.mean(blk * blk, axis=-1, keepdims=True)
        outs.append(blk * lax.rsqrt(ms + EPS) * w[:, g * HEAD_DIM:(g + 1) * HEAD_DIM])
    return jnp.concatenate(outs, axis=1)


def _ep_sigmoid(accs, extra):
    return _sigmoid(accs[0])


def _ep_merge(accs, extra):
    return extra[0] * accs[0] + extra[1] * accs[1]


def _ep_residual(accs, extra):
    return extra[0] + extra[1] * accs[0]


def _ep_swiglu(accs, extra):
    return _silu(accs[0]) * accs[1]


def _sb_body(q_ref, k_ref, v_ref, o_ref, acc_ref, carry_ref, *, tq, tk, hb):
    i = pl.program_id(1)
    ratio = tq // tk
    row = lax.broadcasted_iota(jnp.int32, (tq, tk), 0)
    col = lax.broadcasted_iota(jnp.int32, (tq, tk), 1)
    suffix = (lax.broadcasted_iota(jnp.int32, (tk, tk), 0)
              > lax.broadcasted_iota(jnp.int32, (tk, tk), 1)).astype(BF16)
    heads = [slice(s * HEAD_DIM, (s + 1) * HEAD_DIM) for s in range(hb)]

    def tile(j, causal):
        start = pl.multiple_of(j * tk, tk)
        zs = [lax.dot_general(q_ref[:, h], k_ref[pl.ds(start, tk), h], (((1,), (1,)), ((), ())),
                              preferred_element_type=F32) for h in heads]
        lss, l1ms = [], []
        for z in zs:
            sp = jnp.log(1.0 + jnp.exp(-jnp.abs(z)))
            ls = jnp.minimum(z, 0.0) - sp
            l1m = ls - z
            if causal is not None:
                l1m = jnp.where(causal, l1m, 0.0)
            lss.append(ls)
            l1ms.append(l1m)
        withins = [jnp.dot(l1m.astype(BF16), suffix, preferred_element_type=F32) for l1m in l1ms]
        ws = []
        for h, ls, l1m, within in zip(heads, lss, l1ms, withins):
            carry = carry_ref[:, h]
            w = jnp.exp(ls + within + jnp.concatenate([carry] * (tk // LANES), axis=1))
            if causal is not None:
                w = jnp.where(causal, w, 0.0)
            ws.append(w.astype(BF16))
            carry_ref[:, h] = carry + jnp.sum(l1m, axis=1, keepdims=True)
        for h, w in zip(heads, ws):
            acc_ref[:, h] += jnp.dot(w, v_ref[pl.ds(start, tk), h], preferred_element_type=F32)

    acc_ref[...] = jnp.zeros_like(acc_ref)
    carry_ref[...] = jnp.zeros_like(carry_ref)
    for dj in reversed(range(ratio)):
        tile(i * ratio + dj, col + dj * tk < row)

    def more(st):
        n, live = st
        return jnp.logical_and(n < i * ratio, live)

    def body(st):
        n, _ = st
        tile(i * ratio - 1 - n, None)
        return n + 1, jnp.max(carry_ref[...]) > -EXP_UNDERFLOW

    lax.while_loop(more, body, (jnp.int32(0), jnp.bool_(True)))
    o_ref[...] = acc_ref[...].astype(o_ref.dtype)


def _sb_attention(qk, v, *, tq=SB_Q_TILE, tk=SB_K_TILE, hb=SB_HEADS_PER_STEP):
    s = v.shape[0]
    groups = N_HEADS // hb
    wide = hb * HEAD_DIM
    return pl.pallas_call(
        functools.partial(_sb_body, tq=tq, tk=tk, hb=hb),
        grid=(groups, s // tq),
        in_specs=[pl.BlockSpec((tq, wide), lambda g, i: (i, g)),
                  pl.BlockSpec((s, wide), lambda g, i: (0, groups + g)),
                  pl.BlockSpec((s, wide), lambda g, i: (0, g))],
        out_specs=pl.BlockSpec((tq, wide), lambda g, i: (i, g)),
        out_shape=jax.ShapeDtypeStruct((s, N_HEADS * HEAD_DIM), BF16),
        scratch_shapes=[pltpu.VMEM((tq, wide), F32), pltpu.VMEM((tq, wide), F32)],
        compiler_params=_params("parallel", "arbitrary"),
        name="stick_breaking",
    )(qk, qk, v)


def _gdn_body(q_ref, k_ref, v_ref, ba_ref, z_ref, cw_ref, alog_ref, dtb_ref, onw_ref, o_ref,
              state_ref, tail_ref, gct_ref, *, c, hb):
    grp = pl.program_id(0)
    step = pl.program_id(1)
    heads = range(hb)

    @pl.when(step == 0)
    def _():
        state_ref[...] = jnp.zeros_like(state_ref)
        tail_ref[...] = jnp.zeros_like(tail_ref)

    row = lax.broadcasted_iota(jnp.int32, (c, c), 0)
    col = lax.broadcasted_iota(jnp.int32, (c, c), 1)
    lane = lax.broadcasted_iota(jnp.int32, (1, LANES), 1)
    lower = row >= col
    eye = (row == col).astype(F32)

    ba = ba_ref[...]
    beta_all = _sigmoid(ba)
    g_all = -jnp.exp(alog_ref[...]) * _softplus(ba + dtb_ref[...])
    gc_all = jnp.dot(lower.astype(F32), g_all, precision=lax.Precision.HIGHEST,
                     preferred_element_type=F32)
    gct_ref[...] = gc_all.T

    def conv_silu(x_ref, sec, s):
        cur = x_ref[:, s * HEAD_DIM:(s + 1) * HEAD_DIM]
        lanes = slice((sec * hb + s) * HEAD_DIM, (sec * hb + s + 1) * HEAD_DIM)
        ext = jnp.concatenate([tail_ref[:, lanes], cur], axis=0)
        w = cw_ref[:, lanes]
        y = cur * w[GDN_CONV - 1:GDN_CONV, :]
        for j in range(GDN_CONV - 1):
            shifted = pltpu.roll(ext, GDN_CONV - 1 - j, axis=0)[SUBLANES:, :]
            y = y + shifted * w[j:j + 1, :]
        tail_ref[:, lanes] = cur[c - SUBLANES:, :]
        return _silu(y)

    def l2norm(x):
        return x * lax.rsqrt(jnp.sum(x * x, axis=-1, keepdims=True) + EPS)

    def column(x, idx):
        picked = jnp.sum(jnp.where(lane == idx, x, 0.0), axis=1, keepdims=True)
        return jnp.broadcast_to(picked, (c, LANES))

    qs = [l2norm(conv_silu(q_ref, 0, s)) * (HEAD_DIM ** -0.5) for s in heads]
    ks = [l2norm(conv_silu(k_ref, 1, s)) for s in heads]
    vs = [conv_silu(v_ref, 2, s) for s in heads]
    betas = [column(beta_all, grp * hb + s) for s in heads]
    gcs = [column(gc_all, N_HEADS + grp * hb + s) for s in heads]
    gc_rows = [gct_ref[pl.ds(N_HEADS + grp * hb + s, 1), :] for s in heads]
    decays = [jnp.where(lower, jnp.exp(jnp.where(lower, gc - gc_row, 0.0)), 0.0)
              for gc, gc_row in zip(gcs, gc_rows)]

    kks = [_bdot_nt(k, k) for k in ks]
    qks = [_bdot_nt(q, k) for q, k in zip(qs, ks)]
    attns = [qk * decay for qk, decay in zip(qks, decays)]
    stricts = [jnp.where(row > col, beta * kk * decay, 0.0)
               for beta, kk, decay in zip(betas, kks, decays)]

    invs = [eye - jnp.where((row >> 1) == (col >> 1), st, 0.0) for st in stricts]
    lvl = 1
    while (1 << lvl) < c:
        in_pair = ((row >> (lvl + 1)) == (col >> (lvl + 1))) & ((row >> lvl) != (col >> lvl))
        offs = [jnp.where(in_pair, st, 0.0).astype(BF16) for st in stricts]
        inv16 = [inv.astype(BF16) for inv in invs]
        left = [jnp.dot(a, b, preferred_element_type=F32) for a, b in zip(inv16, offs)]
        corr = [jnp.dot(a.astype(BF16), b, preferred_element_type=F32) for a, b in zip(left, inv16)]
        invs = [inv - cr for inv, cr in zip(invs, corr)]
        lvl += 1

    exp_gcs = [jnp.exp(gc) for gc in gcs]
    rhss = [jnp.concatenate([beta * v, beta * eg * k], axis=1)
            for beta, v, eg, k in zip(betas, vs, exp_gcs, ks)]
    w_vks = [_bdot(inv, rhs) for inv, rhs in zip(invs, rhss)]

    states = [state_ref[s] for s in heads]
    us = [w_vk[:, :HEAD_DIM] - _bdot(w_vk[:, HEAD_DIM:], st) for w_vk, st in zip(w_vks, states)]
    inter = [_bdot(q * eg, st) for q, eg, st in zip(qs, exp_gcs, states)]
    intra = [_bdot(attn, u) for attn, u in zip(attns, us)]
    gc_lasts = [gc[c - 1:c, :] for gc in gcs]
    k_decs = [k * jnp.exp(gl - gc) for k, gl, gc in zip(ks, gc_lasts, gcs)]
    upd = [_bdot(kd.T, u) for kd, u in zip(k_decs, us)]
    for s in heads:
        state_ref[s] = jnp.exp(gc_lasts[s]) * states[s] + upd[s]
        o = inter[s] + intra[s]
        ms = jnp.mean(o * o, axis=-1, keepdims=True)
        zh = z_ref[:, s * HEAD_DIM:(s + 1) * HEAD_DIM]
        o_ref[:, s * HEAD_DIM:(s + 1) * HEAD_DIM] = (
            o * lax.rsqrt(ms + EPS) * onw_ref[...] * _silu(zh)).astype(o_ref.dtype)


def _gdn(qkv, ba, z, conv_w, alog_row, dtb_row, onw_row, *, c=GDN_CHUNK, hb=GDN_HEADS_PER_STEP):
    s = qkv.shape[0]
    groups = N_HEADS // hb
    wide = hb * HEAD_DIM
    sec = lambda k: pl.BlockSpec((c, wide), lambda g, t, k=k: (t, k * groups + g))
    cw = conv_w.reshape(GDN_CONV, 3, groups, wide).transpose(2, 0, 1, 3).reshape(groups, GDN_CONV, 3 * wide)
    row = pl.BlockSpec((1, LANES), lambda g, t: (0, 0))
    return pl.pallas_call(
        functools.partial(_gdn_body, c=c, hb=hb),
        grid=(groups, s // c),
        in_specs=[sec(0), sec(1), sec(2),
                  pl.BlockSpec((c, LANES), lambda g, t: (t, 0)),
                  pl.BlockSpec((c, wide), lambda g, t: (t, g)),
                  pl.BlockSpec((None, GDN_CONV, 3 * wide), lambda g, t: (g, 0, 0)),
                  row, row, row],
        out_specs=pl.BlockSpec((c, wide), lambda g, t: (t, g)),
        out_shape=jax.ShapeDtypeStruct((s, N_HEADS * HEAD_DIM), BF16),
        scratch_shapes=[pltpu.VMEM((hb, HEAD_DIM, HEAD_DIM), F32),
                        pltpu.VMEM((SUBLANES, 3 * wide), F32),
                        pltpu.VMEM((LANES, c), F32)],
        compiler_params=_params("parallel", "arbitrary"),
        name="gated_deltanet",
    )(qkv, qkv, qkv, ba, z, cw, alog_row, dtb_row, onw_row)


def kernel(x, c, w_mod, b_mod, norm1_w, w_in, q_norm_w, k_norm_w, conv_w, a_log, dt_bias,
           o_norm_w, p_a, p_b, w_out, norm2_w, w_gate, w_up, w_down):
    bsz, s, d = x.shape
    depth = w_mod.shape[0]
    assert bsz == 1, "single-sequence prefill block"
    hd = N_HEADS * HEAD_DIM
    d_ff = w_gate.shape[2]
    h = x[0].astype(F32)
    c_col = c.astype(F32).reshape(d, 1)

    o_v, o_qkvb, o_z = 2 * hd, 3 * hd, 6 * hd
    o_b, o_gates = 7 * hd, 7 * hd + 2 * N_HEADS

    for l in range(depth):
        mod = _adaln(c_col, w_mod[l].astype(F32), b_mod[l].astype(F32).reshape(1, 6 * d))
        shift1, scale1, gate1, shift2, scale2, gate2 = [mod[:, i * d:(i + 1) * d] for i in range(6)]

        w_main = w_in[l][:, :o_b].astype(BF16)
        w_ba = jnp.pad(w_in[l][:, o_b:o_gates], ((0, 0), (0, LANES - 2 * N_HEADS))).astype(BF16)
        w_gates = w_in[l][:, o_gates:].astype(BF16)

        u = _norm_mod(h, norm1_w[l].astype(F32).reshape(1, d), scale1, shift1)

        tm, tn = 512, 1024
        qk_norm_w = jnp.concatenate([jnp.tile(q_norm_w[l].astype(F32) * (HEAD_DIM ** -0.5), N_HEADS),
                                     jnp.tile(k_norm_w[l].astype(F32), N_HEADS)]).reshape(1, 2 * hd)
        qk_a = _matmul([(u, w_main, 0)], [(qk_norm_w, "row", 0)], _ep_head_rmsnorm, 2 * hd, BF16,
                       tm=tm, tn=tn, name="proj_qk")
        v_a = _matmul([(u, w_main, o_v // tn)], [], _ep_plain, hd, BF16, tm=tm, tn=tn, name="proj_v")
        qkv_b = _matmul([(u, w_main, o_qkvb // tn)], [], _ep_plain, 3 * hd, F32, tm=tm, tn=tn,
                        name="proj_qkv_b")
        z_b = _matmul([(u, w_main, o_z // tn)], [], _ep_plain, hd, F32, tm=tm, tn=tn, name="proj_z")
        ba = _matmul([(u, w_ba, 0)], [], _ep_plain, LANES, F32, tm=tm, tn=LANES, name="proj_ba")
        sig_gates = _matmul([(u, w_gates, 0)], [], _ep_sigmoid, 2 * d, BF16, tm=tm, tn=tn,
                            name="proj_gates")

        o_a = _sb_attention(qk_a, v_a)

        pad = (0, LANES - 2 * N_HEADS)
        alog_row = jnp.pad(jnp.concatenate([jnp.zeros((N_HEADS,), F32), a_log[l].astype(F32)]), pad)
        dtb_row = jnp.pad(jnp.concatenate([jnp.zeros((N_HEADS,), F32), dt_bias[l].astype(F32)]), pad)
        o_b_out = _gdn(qkv_b, ba, z_b, conv_w[l].astype(F32), alog_row.reshape(1, LANES),
                       dtb_row.reshape(1, LANES), o_norm_w[l].astype(F32).reshape(1, HEAD_DIM))

        merged = _matmul([(o_a, p_a[l].astype(BF16), 0), (o_b_out, p_b[l].astype(BF16), 0)],
                         [(sig_gates, "tile", 0), (sig_gates, "tile", d // tn)], _ep_merge, d, BF16,
                         tm=tm, tn=tn, name="merge_proj")
        h = _matmul([(merged, w_out[l].astype(BF16), 0)], [(h, "tile", 0), (gate1, "row", 0)],
                    _ep_residual, d, F32, tm=tm, tn=tn, name="out_proj")

        u2 = _norm_mod(h, norm2_w[l].astype(F32).reshape(1, d), scale2, shift2)
        tf = 512
        ff = _matmul([(u2, w_gate[l].astype(BF16), 0), (u2, w_up[l].astype(BF16), 0)], [],
                     _ep_swiglu, d_ff, BF16, tm=tm, tn=tf, name="ffn_up")
        h = _matmul([(ff, w_down[l].astype(BF16), 0)], [(h, "tile", 0), (gate2, "row", 0)],
                    _ep_residual, d, F32, tm=tm, tn=tn, name="ffn_down")
    return h.reshape(bsz, s, d).astype(x.dtype)
```

```python
import functools

import jax
import jax.numpy as jnp
from jax import lax
from jax.experimental import pallas as pl
from jax.experimental.pallas import tpu as pltpu

F32 = jnp.float32
BF16 = jnp.bfloat16

EPS = 1e-6
EXP_UNDERFLOW = 105.0
HEAD_DIM = 128
N_HEADS = 16
GDN_CONV = 4
LANES = 128
SUBLANES = 8
VMEM_LIMIT_BYTES = 56 * 1024 * 1024

MM_ROW_CHUNK = 512
SB_K_TILE = 256
SB_Q_TILE = 256
SB_HEADS_PER_STEP = 4
GDN_CHUNK = 128
GDN_HEADS_PER_STEP = 16


def _params(*sem):
    return pltpu.CompilerParams(dimension_semantics=sem, vmem_limit_bytes=VMEM_LIMIT_BYTES)


def _sigmoid(x):
    return 1.0 / (1.0 + jnp.exp(-x))


def _silu(x):
    return x * _sigmoid(x)


def _softplus(x):
    return jnp.maximum(x, 0.0) + jnp.log1p(jnp.exp(-jnp.abs(x)))


def _bdot(a, b):
    return jnp.dot(a.astype(BF16), b.astype(BF16), preferred_element_type=F32)


def _bdot_nt(a, b):
    return lax.dot_general(a.astype(BF16), b.astype(BF16), (((1,), (1,)), ((), ())),
                           preferred_element_type=F32)


def _adaln_body(c_ref, w_ref, b_ref, o_ref):
    c = c_ref[...]
    o_ref[...] = jnp.sum(w_ref[...] * _silu(c), axis=0, keepdims=True) + b_ref[...]


def _adaln(c_col, w, b, *, tn=1024):
    d, n = w.shape
    return pl.pallas_call(
        _adaln_body,
        grid=(n // tn,),
        in_specs=[pl.BlockSpec((d, 1), lambda j: (0, 0)),
                  pl.BlockSpec((d, tn), lambda j: (0, j)),
                  pl.BlockSpec((1, tn), lambda j: (0, j))],
        out_specs=pl.BlockSpec((1, tn), lambda j: (0, j)),
        out_shape=jax.ShapeDtypeStruct((1, n), F32),
        compiler_params=_params("parallel"),
        name="adaln_mod",
    )(c_col, w, b)


def _norm_mod_body(x_ref, w_ref, scale_ref, shift_ref, o_ref):
    x = x_ref[...]
    y = x * lax.rsqrt(jnp.mean(x * x, axis=-1, keepdims=True) + EPS) * w_ref[...]
    o_ref[...] = (y * (1.0 + scale_ref[...]) + shift_ref[...]).astype(o_ref.dtype)


def _norm_mod(x, w, scale, shift, *, tm=512):
    s, d = x.shape
    row = pl.BlockSpec((1, d), lambda i: (0, 0))
    return pl.pallas_call(
        _norm_mod_body,
        grid=(s // tm,),
        in_specs=[pl.BlockSpec((tm, d), lambda i: (i, 0)), row, row, row],
        out_specs=pl.BlockSpec((tm, d), lambda i: (i, 0)),
        out_shape=jax.ShapeDtypeStruct((s, d), BF16),
        compiler_params=_params("parallel"),
        name="norm_mod",
    )(x, w, scale, shift)


def _mm_body(*refs, n_pairs, shifts, n_extra, epilogue, tn):
    a_refs, pos = refs[:n_pairs], n_pairs
    weights = []
    for shift in shifts:
        if shift:
            w = jnp.concatenate([refs[pos][...], refs[pos + 1][...]], axis=1)[:, shift:shift + tn]
            pos += 2
        else:
            w = refs[pos][...]
            pos += 1
        weights.append(w.astype(BF16))
    extra, o_ref = refs[pos:pos + n_extra], refs[pos + n_extra]
    tm = o_ref.shape[0]
    rows = min(tm, MM_ROW_CHUNK)
    for r in range(tm // rows):
        rs = slice(r * rows, (r + 1) * rows)
        accs = [jnp.dot(a[rs, :], w, preferred_element_type=F32) for a, w in zip(a_refs, weights)]
        ex = [e[rs, :] if e.shape[0] == tm else e[...] for e in extra]
        o_ref[rs, :] = epilogue(accs, ex).astype(o_ref.dtype)


def _matmul(pairs, extras, epilogue, n, out_dtype, *, tm, tn, name):
    m = pairs[0][0].shape[0]
    tm = min(tm, m)
    in_specs, args, shifts = [], [], []
    for a, _, _ in pairs:
        in_specs.append(pl.BlockSpec((tm, a.shape[1]), lambda i, j: (i, 0)))
        args.append(a)
    for _, b, col0 in pairs:
        shift = col0 % LANES
        base = col0 - shift
        assert base % tn == 0 and tn % LANES == 0
        in_specs.append(pl.BlockSpec((b.shape[0], tn), lambda i, j, off=base // tn: (0, j + off)))
        args.append(b)
        if shift:
            in_specs.append(pl.BlockSpec(
                (b.shape[0], LANES),
                lambda i, j, off=base // LANES, step=tn // LANES: (0, off + (j + 1) * step)))
            args.append(b)
        shifts.append(shift)
    for arr, kind, off in extras:
        if kind == "tile":
            in_specs.append(pl.BlockSpec((tm, tn), lambda i, j, off=off: (i, j + off)))
        else:
            in_specs.append(pl.BlockSpec((1, tn), lambda i, j: (0, j)))
        args.append(arr)
    body = functools.partial(_mm_body, n_pairs=len(pairs), shifts=tuple(shifts), n_extra=len(extras),
                             epilogue=epilogue, tn=tn)
    return pl.pallas_call(
        body,
        grid=(m // tm, n // tn),
        in_specs=in_specs,
        out_specs=pl.BlockSpec((tm, tn), lambda i, j: (i, j)),
        out_shape=jax.ShapeDtypeStruct((m, n), out_dtype),
        compiler_params=_params("parallel", "parallel"),
        name=name,
    )(*args)


def _ep_plain(accs, extra):
    return accs[0]


def _ep_head_rmsnorm(accs, extra):
    acc, w = accs[0], extra[0]
    outs = []
    for g in range(acc.shape[1] // HEAD_DIM):
        blk = acc[:, g * HEAD_DIM:(g + 1) * HEAD_DIM]
        ms = jnp.mean(blk * blk, axis=-1, keepdims=True)
        outs.append(blk * lax.rsqrt(ms + EPS) * w[:, g * HEAD_DIM:(g + 1) * HEAD_DIM])
    return jnp.concatenate(outs, axis=1)


def _ep_sigmoid(accs, extra):
    return _sigmoid(accs[0])


def _ep_merge(accs, extra):
    return extra[0] * accs[0] + extra[1] * accs[1]


def _ep_residual(accs, extra):
    return extra[0] + extra[1] * accs[0]


def _ep_swiglu(accs, extra):
    return _silu(accs[0]) * accs[1]


def _sb_body(q_ref, k_ref, v_ref, o_ref, acc_ref, carry_ref, *, tq, tk, hb):
    i = pl.program_id(1)
    ratio = tq // tk
    row = lax.broadcasted_iota(jnp.int32, (tq, tk), 0)
    col = lax.broadcasted_iota(jnp.int32, (tq, tk), 1)
    suffix = (lax.broadcasted_iota(jnp.int32, (tk, tk), 0)
              > lax.broadcasted_iota(jnp.int32, (tk, tk), 1)).astype(BF16)
    heads = [slice(s * HEAD_DIM, (s + 1) * HEAD_DIM) for s in range(hb)]

    def tile(j, causal):
        start = pl.multiple_of(j * tk, tk)
        zs = [lax.dot_general(q_ref[:, h], k_ref[pl.ds(start, tk), h], (((1,), (1,)), ((), ())),
                              preferred_element_type=F32) for h in heads]
        lss, l1ms = [], []
        for z in zs:
            sp = jnp.log(1.0 + jnp.exp(-jnp.abs(z)))
            ls = jnp.minimum(z, 0.0) - sp
            l1m = ls - z
            if causal is not None:
                l1m = jnp.where(causal, l1m, 0.0)
            lss.append(ls)
            l1ms.append(l1m)
        withins = [jnp.dot(l1m.astype(BF16), suffix, preferred_element_type=F32) for l1m in l1ms]
        ws = []
        for h, ls, l1m, within in zip(heads, lss, l1ms, withins):
            carry = carry_ref[:, h]
            w = jnp.exp(ls + within + jnp.concatenate([carry] * (tk // LANES), axis=1))
            if causal is not None:
                w = jnp.where(causal, w, 0.0)
            ws.append(w.astype(BF16))
            carry_ref[:, h] = carry + jnp.sum(l1m, axis=1, keepdims=True)
        for h, w in zip(heads, ws):
            acc_ref[:, h] += jnp.dot(w, v_ref[pl.ds(start, tk), h], preferred_element_type=F32)

    acc_ref[...] = jnp.zeros_like(acc_ref)
    carry_ref[...] = jnp.zeros_like(carry_ref)
    for dj in reversed(range(ratio)):
        tile(i * ratio + dj, col + dj * tk < row)

    def more(st):
        n, live = st
        return jnp.logical_and(n < i * ratio, live)

    def body(st):
        n, _ = st
        tile(i * ratio - 1 - n, None)
        return n + 1, jnp.max(carry_ref[...]) > -EXP_UNDERFLOW

    lax.while_loop(more, body, (jnp.int32(0), jnp.bool_(True)))
    o_ref[...] = acc_ref[...].astype(o_ref.dtype)


def _sb_attention(qk, v, *, tq=SB_Q_TILE, tk=SB_K_TILE, hb=SB_HEADS_PER_STEP):
    s = v.shape[0]
    groups = N_HEADS // hb
    wide = hb * HEAD_DIM
    return pl.pallas_call(
        functools.partial(_sb_body, tq=tq, tk=tk, hb=hb),
        grid=(groups, s // tq),
        in_specs=[pl.BlockSpec((tq, wide), lambda g, i: (i, g)),
                  pl.BlockSpec((s, wide), lambda g, i: (0, groups + g)),
                  pl.BlockSpec((s, wide), lambda g, i: (0, g))],
        out_specs=pl.BlockSpec((tq, wide), lambda g, i: (i, g)),
        out_shape=jax.ShapeDtypeStruct((s, N_HEADS * HEAD_DIM), BF16),
        scratch_shapes=[pltpu.VMEM((tq, wide), F32), pltpu.VMEM((tq, wide), F32)],
        compiler_params=_params("parallel", "arbitrary"),
        name="stick_breaking",
    )(qk, qk, v)


def _gdn_body(q_ref, k_ref, v_ref, ba_ref, z_ref, cw_ref, alog_ref, dtb_ref, onw_ref, o_ref,
              state_ref, tail_ref, gct_ref, *, c, hb):
    grp = pl.program_id(0)
    step = pl.program_id(1)
    heads = range(hb)

    @pl.when(step == 0)
    def _():
        state_ref[...] = jnp.zeros_like(state_ref)
        tail_ref[...] = jnp.zeros_like(tail_ref)

    row = lax.broadcasted_iota(jnp.int32, (c, c), 0)
    col = lax.broadcasted_iota(jnp.int32, (c, c), 1)
    lane = lax.broadcasted_iota(jnp.int32, (1, LANES), 1)
    lower = row >= col
    eye = (row == col).astype(F32)

    ba = ba_ref[...]
    beta_all = _sigmoid(ba)
    g_all = -jnp.exp(alog_ref[...]) * _softplus(ba + dtb_ref[...])
    gc_all = jnp.dot(lower.astype(F32), g_all, precision=lax.Precision.HIGHEST,
                     preferred_element_type=F32)
    gct_ref[...] = gc_all.T

    def conv_silu(x_ref, sec, s):
        cur = x_ref[:, s * HEAD_DIM:(s + 1) * HEAD_DIM].astype(F32)
        lanes = slice((sec * hb + s) * HEAD_DIM, (sec * hb + s + 1) * HEAD_DIM)
        ext = jnp.concatenate([tail_ref[:, lanes], cur], axis=0)
        w = cw_ref[:, lanes]
        y = cur * w[GDN_CONV - 1:GDN_CONV, :]
        for j in range(GDN_CONV - 1):
            shifted = pltpu.roll(ext, GDN_CONV - 1 - j, axis=0)[SUBLANES:, :]
            y = y + shifted * w[j:j + 1, :]
        tail_ref[:, lanes] = cur[c - SUBLANES:, :]
        return _silu(y)

    def l2norm(x):
        return x * lax.rsqrt(jnp.sum(x * x, axis=-1, keepdims=True) + EPS)

    def column(x, idx):
        picked = jnp.sum(jnp.where(lane == idx, x, 0.0), axis=1, keepdims=True)
        return jnp.broadcast_to(picked, (c, LANES))

    qs = [l2norm(conv_silu(q_ref, 0, s)) * (HEAD_DIM ** -0.5) for s in heads]
    ks = [l2norm(conv_silu(k_ref, 1, s)) for s in heads]
    vs = [conv_silu(v_ref, 2, s) for s in heads]
    betas = [column(beta_all, grp * hb + s) for s in heads]
    gcs = [column(gc_all, N_HEADS + grp * hb + s) for s in heads]
    gc_rows = [gct_ref[pl.ds(N_HEADS + grp * hb + s, 1), :] for s in heads]
    decays = [jnp.where(lower, jnp.exp(jnp.where(lower, gc - gc_row, 0.0)), 0.0)
              for gc, gc_row in zip(gcs, gc_rows)]

    kks = [_bdot_nt(k, k) for k in ks]
    qks = [_bdot_nt(q, k) for q, k in zip(qs, ks)]
    attns = [qk * decay for qk, decay in zip(qks, decays)]
    stricts = [jnp.where(row > col, beta * kk * decay, 0.0)
               for beta, kk, decay in zip(betas, kks, decays)]

    invs = [eye - jnp.where((row >> 1) == (col >> 1), st, 0.0) for st in stricts]
    lvl = 1
    while (1 << lvl) < c:
        in_pair = ((row >> (lvl + 1)) == (col >> (lvl + 1))) & ((row >> lvl) != (col >> lvl))
        offs = [jnp.where(in_pair, st, 0.0).astype(BF16) for st in stricts]
        inv16 = [inv.astype(BF16) for inv in invs]
        left = [jnp.dot(a, b, preferred_element_type=F32) for a, b in zip(inv16, offs)]
        corr = [jnp.dot(a.astype(BF16), b, preferred_element_type=F32) for a, b in zip(left, inv16)]
        invs = [inv - cr for inv, cr in zip(invs, corr)]
        lvl += 1

    exp_gcs = [jnp.exp(gc) for gc in gcs]
    rhss = [jnp.concatenate([beta * v, beta * eg * k], axis=1)
            for beta, v, eg, k in zip(betas, vs, exp_gcs, ks)]
    w_vks = [_bdot(inv, rhs) for inv, rhs in zip(invs, rhss)]

    states = [state_ref[s] for s in heads]
    us = [w_vk[:, :HEAD_DIM] - _bdot(w_vk[:, HEAD_DIM:], st) for w_vk, st in zip(w_vks, states)]
    inter = [_bdot(q * eg, st) for q, eg, st in zip(qs, exp_gcs, states)]
    intra = [_bdot(attn, u) for attn, u in zip(attns, us)]
    gc_lasts = [gc[c - 1:c, :] for gc in gcs]
    k_decs = [k * jnp.exp(gl - gc) for k, gl, gc in zip(ks, gc_lasts, gcs)]
    upd = [_bdot(kd.T, u) for kd, u in zip(k_decs, us)]
    for s in heads:
        state_ref[s] = jnp.exp(gc_lasts[s]) * states[s] + upd[s]
        o = inter[s] + intra[s]
        ms = jnp.mean(o * o, axis=-1, keepdims=True)
        zh = z_ref[:, s * HEAD_DIM:(s + 1) * HEAD_DIM].astype(F32)
        o_ref[:, s * HEAD_DIM:(s + 1) * HEAD_DIM] = (
            o * lax.rsqrt(ms + EPS) * onw_ref[...] * _silu(zh)).astype(o_ref.dtype)


def _gdn(qkv, ba, z, conv_w, alog_row, dtb_row, onw_row, *, c=GDN_CHUNK, hb=GDN_HEADS_PER_STEP):
    s = qkv.shape[0]
    groups = N_HEADS // hb
    wide = hb * HEAD_DIM
    sec = lambda k: pl.BlockSpec((c, wide), lambda g, t, k=k: (t, k * groups + g))
    cw = conv_w.reshape(GDN_CONV, 3, groups, wide).transpose(2, 0, 1, 3).reshape(groups, GDN_CONV, 3 * wide)
    row = pl.BlockSpec((1, LANES), lambda g, t: (0, 0))
    return pl.pallas_call(
        functools.partial(_gdn_body, c=c, hb=hb),
        grid=(groups, s // c),
        in_specs=[sec(0), sec(1), sec(2),
                  pl.BlockSpec((c, LANES), lambda g, t: (t, 0)),
                  pl.BlockSpec((c, wide), lambda g, t: (t, g)),
                  pl.BlockSpec((None, GDN_CONV, 3 * wide), lambda g, t: (g, 0, 0)),
                  row, row, row],
        out_specs=pl.BlockSpec((c, wide), lambda g, t: (t, g)),
        out_shape=jax.ShapeDtypeStruct((s, N_HEADS * HEAD_DIM), BF16),
        scratch_shapes=[pltpu.VMEM((hb, HEAD_DIM, HEAD_DIM), F32),
                        pltpu.VMEM((SUBLANES, 3 * wide), F32),
                        pltpu.VMEM((LANES, c), F32)],
        compiler_params=_params("parallel", "arbitrary"),
        name="gated_deltanet",
    )(qkv, qkv, qkv, ba, z, cw, alog_row, dtb_row, onw_row)


def kernel(x, c, w_mod, b_mod, norm1_w, w_in, q_norm_w, k_norm_w, conv_w, a_log, dt_bias,
           o_norm_w, p_a, p_b, w_out, norm2_w, w_gate, w_up, w_down):
    bsz, s, d = x.shape
    depth = w_mod.shape[0]
    assert bsz == 1, "single-sequence prefill block"
    hd = N_HEADS * HEAD_DIM
    d_ff = w_gate.shape[2]
    h = x[0].astype(F32)
    c_col = c.astype(F32).reshape(d, 1)

    o_v, o_qkvb, o_z = 2 * hd, 3 * hd, 6 * hd
    o_b, o_gates = 7 * hd, 7 * hd + 2 * N_HEADS

    for l in range(depth):
        mod = _adaln(c_col, w_mod[l].astype(F32), b_mod[l].astype(F32).reshape(1, 6 * d))
        shift1, scale1, gate1, shift2, scale2, gate2 = [mod[:, i * d:(i + 1) * d] for i in range(6)]

        u = _norm_mod(h, norm1_w[l].astype(F32).reshape(1, d), scale1, shift1)

        w_l = w_in[l]
        tm, tn = 2048, 512
        qk_norm_w = jnp.concatenate([jnp.tile(q_norm_w[l].astype(F32) * (HEAD_DIM ** -0.5), N_HEADS),
                                     jnp.tile(k_norm_w[l].astype(F32), N_HEADS)]).reshape(1, 2 * hd)
        qk_a = _matmul([(u, w_l, 0)], [(qk_norm_w, "row", 0)], _ep_head_rmsnorm, 2 * hd, BF16,
                       tm=tm, tn=tn, name="proj_qk")
        v_a = _matmul([(u, w_l, o_v)], [], _ep_plain, hd, BF16, tm=tm, tn=tn, name="proj_v")
        qkv_b = _matmul([(u, w_l, o_qkvb)], [], _ep_plain, 3 * hd, BF16, tm=tm, tn=tn,
                        name="proj_qkv_b")
        z_b = _matmul([(u, w_l, o_z)], [], _ep_plain, hd, BF16, tm=tm, tn=tn, name="proj_z")
        ba = _matmul([(u, w_l, o_b)], [], _ep_plain, LANES, F32, tm=tm, tn=LANES, name="proj_ba")
        sig_gates = _matmul([(u, w_l, o_gates)], [], _ep_sigmoid, 2 * d, BF16, tm=tm, tn=tn,
                            name="proj_gates")

        o_a = _sb_attention(qk_a, v_a)

        pad = (0, LANES - 2 * N_HEADS)
        alog_row = jnp.pad(jnp.concatenate([jnp.zeros((N_HEADS,), F32), a_log[l].astype(F32)]), pad)
        dtb_row = jnp.pad(jnp.concatenate([jnp.zeros((N_HEADS,), F32), dt_bias[l].astype(F32)]), pad)
        o_b_out = _gdn(qkv_b, ba, z_b, conv_w[l].astype(F32), alog_row.reshape(1, LANES),
                       dtb_row.reshape(1, LANES), o_norm_w[l].astype(F32).reshape(1, HEAD_DIM))

        tm, tn = 1024, 512
        merged = _matmul([(o_a, p_a[l], 0), (o_b_out, p_b[l], 0)],
                         [(sig_gates, "tile", 0), (sig_gates, "tile", d // tn)], _ep_merge, d, BF16,
                         tm=tm, tn=tn, name="merge_proj")
        h = _matmul([(merged, w_out[l], 0)], [(h, "tile", 0), (gate1, "row", 0)],
                    _ep_residual, d, F32, tm=tm, tn=tn, name="out_proj")

        u2 = _norm_mod(h, norm2_w[l].astype(F32).reshape(1, d), scale2, shift2)
        ff = _matmul([(u2, w_gate[l], 0), (u2, w_up[l], 0)], [], _ep_swiglu, d_ff, BF16,
                     tm=2048, tn=256, name="ffn_up")
        h = _matmul([(ff, w_down[l], 0)], [(h, "tile", 0), (gate2, "row", 0)],
                    _ep_residual, d, F32, tm=1024, tn=256, name="ffn_down")
    return h.reshape(bsz, s, d).astype(x.dtype)
```

```python
import functools

import jax
import jax.numpy as jnp
from jax import lax
from jax.experimental import pallas as pl
from jax.experimental.pallas import tpu as pltpu

F32 = jnp.float32
BF16 = jnp.bfloat16

EPS = 1e-6
EXP_UNDERFLOW = 105.0
HEAD_DIM = 128
N_HEADS = 16
GDN_CONV = 4
LANES = 128
SUBLANES = 8
VMEM_LIMIT_BYTES = 56 * 1024 * 1024

MM_ROW_CHUNK = 512
SB_K_TILE = 256
SB_Q_TILE = 256
SB_HEADS_PER_STEP = 4
GDN_CHUNK = 128
GDN_HEADS_PER_STEP = 16


def _params(*sem):
    return pltpu.CompilerParams(dimension_semantics=sem, vmem_limit_bytes=VMEM_LIMIT_BYTES)


def _sigmoid(x):
    return 1.0 / (1.0 + jnp.exp(-x))


def _silu(x):
    return x * _sigmoid(x)


def _softplus(x):
    return jnp.maximum(x, 0.0) + jnp.log1p(jnp.exp(-jnp.abs(x)))


def _bdot(a, b):
    return jnp.dot(a.astype(BF16), b.astype(BF16), preferred_element_type=F32)


def _bdot_nt(a, b):
    return lax.dot_general(a.astype(BF16), b.astype(BF16), (((1,), (1,)), ((), ())),
                           preferred_element_type=F32)


def _adaln_body(c_ref, w_ref, b_ref, o_ref):
    c = c_ref[...]
    o_ref[...] = jnp.sum(w_ref[...] * _silu(c), axis=0, keepdims=True) + b_ref[...]


def _adaln(c_col, w, b, *, tn=1024):
    d, n = w.shape
    return pl.pallas_call(
        _adaln_body,
        grid=(n // tn,),
        in_specs=[pl.BlockSpec((d, 1), lambda j: (0, 0)),
                  pl.BlockSpec((d, tn), lambda j: (0, j)),
                  pl.BlockSpec((1, tn), lambda j: (0, j))],
        out_specs=pl.BlockSpec((1, tn), lambda j: (0, j)),
        out_shape=jax.ShapeDtypeStruct((1, n), F32),
        compiler_params=_params("parallel"),
        name="adaln_mod",
    )(c_col, w, b)


def _norm_mod_body(x_ref, w_ref, scale_ref, shift_ref, o_ref):
    x = x_ref[...]
    y = x * lax.rsqrt(jnp.mean(x * x, axis=-1, keepdims=True) + EPS) * w_ref[...]
    o_ref[...] = (y * (1.0 + scale_ref[...]) + shift_ref[...]).astype(o_ref.dtype)


def _norm_mod(x, w, scale, shift, *, tm=512):
    s, d = x.shape
    row = pl.BlockSpec((1, d), lambda i: (0, 0))
    return pl.pallas_call(
        _norm_mod_body,
        grid=(s // tm,),
        in_specs=[pl.BlockSpec((tm, d), lambda i: (i, 0)), row, row, row],
        out_specs=pl.BlockSpec((tm, d), lambda i: (i, 0)),
        out_shape=jax.ShapeDtypeStruct((s, d), BF16),
        compiler_params=_params("parallel"),
        name="norm_mod",
    )(x, w, scale, shift)


def _mm_body(*refs, n_lhs, kinds, n_extra, epilogue, tn):
    pos = n_lhs
    a_refs = [refs[lhs] for _, _, lhs in kinds]
    weights = []
    for layout, shift, _ in kinds:
        if shift:
            w = jnp.concatenate([refs[pos][...], refs[pos + 1][...]], axis=0)[shift:shift + tn, :]
            pos += 2
        else:
            w = refs[pos][...]
            pos += 1
        weights.append(w.astype(BF16))
    extra, o_ref = refs[pos:pos + n_extra], refs[pos + n_extra]
    tm = o_ref.shape[0]
    rows = min(tm, MM_ROW_CHUNK)
    for r in range(tm // rows):
        rs = slice(r * rows, (r + 1) * rows)
        accs = [lax.dot_general(a[rs, :], w, (((1,), (1 if layout == "nk" else 0,)), ((), ())),
                                preferred_element_type=F32)
                for a, w, (layout, _, _) in zip(a_refs, weights, kinds)]
        ex = [e[rs, :] if e.shape[0] == tm else e[...] for e in extra]
        o_ref[rs, :] = epilogue(accs, ex).astype(o_ref.dtype)


def _matmul(pairs, extras, epilogue, n, out_dtype, *, tm, tn, name):
    m = pairs[0][0].shape[0]
    tm = min(tm, m)
    in_specs, args, kinds = [], [], []
    for a, _, _, _ in pairs:
        if not any(a is seen for seen in args):
            in_specs.append(pl.BlockSpec((tm, a.shape[1]), lambda i, j: (i, 0)))
            args.append(a)
    n_lhs = len(args)
    for a, b, col0, layout in pairs:
        shift = col0 % tn
        base = col0 - shift
        if layout == "kn":
            assert shift == 0
            in_specs.append(pl.BlockSpec((b.shape[0], tn), lambda i, j, off=base // tn: (0, j + off)))
            args.append(b)
        else:
            in_specs.append(pl.BlockSpec((tn, b.shape[1]), lambda i, j, off=base // tn: (j + off, 0)))
            args.append(b)
            if shift:
                assert shift % SUBLANES == 0 and base % shift == 0 and tn % shift == 0
                in_specs.append(pl.BlockSpec(
                    (shift, b.shape[1]),
                    lambda i, j, off=base // shift, step=tn // shift: (off + (j + 1) * step, 0)))
                args.append(b)
        kinds.append((layout, shift, [a is lhs for lhs in args[:n_lhs]].index(True)))
    for arr, kind, off in extras:
        if kind == "tile":
            in_specs.append(pl.BlockSpec((tm, tn), lambda i, j, off=off: (i, j + off)))
        else:
            in_specs.append(pl.BlockSpec((1, tn), lambda i, j: (0, j)))
        args.append(arr)
    body = functools.partial(_mm_body, n_lhs=n_lhs, kinds=tuple(kinds), n_extra=len(extras),
                             epilogue=epilogue, tn=tn)
    return pl.pallas_call(
        body,
        grid=(m // tm, n // tn),
        in_specs=in_specs,
        out_specs=pl.BlockSpec((tm, tn), lambda i, j: (i, j)),
        out_shape=jax.ShapeDtypeStruct((m, n), out_dtype),
        compiler_params=_params("parallel", "parallel"),
        name=name,
    )(*args)


def _ep_plain(accs, extra):
    return accs[0]


def _ep_head_rmsnorm(accs, extra):
    acc, w = accs[0], extra[0]
    outs = []
    for g in range(acc.shape[1] // HEAD_DIM):
        blk = acc[:, g * HEAD_DIM:(g + 1) * HEAD_DIM]
        ms = jnp.mean(blk * blk, axis=-1, keepdims=True)
        outs.append(blk * lax.rsqrt(ms + EPS) * w[:, g * HEAD_DIM:(g + 1) * HEAD_DIM])
    return jnp.concatenate(outs, axis=1)


def _ep_sigmoid(accs, extra):
    return _sigmoid(accs[0])


def _ep_merge(accs, extra):
    return extra[0] * accs[0] + extra[1] * accs[1]


def _ep_residual(accs, extra):
    return extra[0] + extra[1] * accs[0]


def _ep_swiglu(accs, extra):
    return _silu(accs[0]) * accs[1]


def _sb_body(q_ref, k_ref, v_ref, o_ref, acc_ref, carry_ref, *, tq, tk, hb):
    i = pl.program_id(1)
    ratio = tq // tk
    row = lax.broadcasted_iota(jnp.int32, (tq, tk), 0)
    col = lax.broadcasted_iota(jnp.int32, (tq, tk), 1)
    suffix = (lax.broadcasted_iota(jnp.int32, (tk, tk), 0)
              > lax.broadcasted_iota(jnp.int32, (tk, tk), 1)).astype(BF16)
    heads = [slice(s * HEAD_DIM, (s + 1) * HEAD_DIM) for s in range(hb)]

    def tile(j, causal):
        start = pl.multiple_of(j * tk, tk)
        zs = [lax.dot_general(q_ref[:, h], k_ref[pl.ds(start, tk), h], (((1,), (1,)), ((), ())),
                              preferred_element_type=F32) for h in heads]
        lss, l1ms = [], []
        for z in zs:
            sp = jnp.log(1.0 + jnp.exp(-jnp.abs(z)))
            ls = jnp.minimum(z, 0.0) - sp
            l1m = ls - z
            if causal is not None:
                l1m = jnp.where(causal, l1m, 0.0)
            lss.append(ls)
            l1ms.append(l1m)
        withins = [jnp.dot(l1m.astype(BF16), suffix, preferred_element_type=F32) for l1m in l1ms]
        ws = []
        for h, ls, l1m, within in zip(heads, lss, l1ms, withins):
            carry = carry_ref[:, h]
            w = jnp.exp(ls + within + jnp.concatenate([carry] * (tk // LANES), axis=1))
            if causal is not None:
                w = jnp.where(causal, w, 0.0)
            ws.append(w.astype(BF16))
            carry_ref[:, h] = carry + jnp.sum(l1m, axis=1, keepdims=True)
        for h, w in zip(heads, ws):
            acc_ref[:, h] += jnp.dot(w, v_ref[pl.ds(start, tk), h], preferred_element_type=F32)

    acc_ref[...] = jnp.zeros_like(acc_ref)
    carry_ref[...] = jnp.zeros_like(carry_ref)
    for dj in reversed(range(ratio)):
        tile(i * ratio + dj, col + dj * tk < row)

    def more(st):
        n, live = st
        return jnp.logical_and(n < i * ratio, live)

    def body(st):
        n, _ = st
        tile(i * ratio - 1 - n, None)
        return n + 1, jnp.max(carry_ref[...]) > -EXP_UNDERFLOW

    lax.while_loop(more, body, (jnp.int32(0), jnp.bool_(True)))
    o_ref[...] = acc_ref[...].astype(o_ref.dtype)


def _sb_attention(qk, v, *, tq=SB_Q_TILE, tk=SB_K_TILE, hb=SB_HEADS_PER_STEP):
    s = v.shape[0]
    groups = N_HEADS // hb
    wide = hb * HEAD_DIM
    return pl.pallas_call(
        functools.partial(_sb_body, tq=tq, tk=tk, hb=hb),
        grid=(groups, s // tq),
        in_specs=[pl.BlockSpec((tq, wide), lambda g, i: (i, g)),
                  pl.BlockSpec((s, wide), lambda g, i: (0, groups + g)),
                  pl.BlockSpec((s, wide), lambda g, i: (0, g))],
        out_specs=pl.BlockSpec((tq, wide), lambda g, i: (i, g)),
        out_shape=jax.ShapeDtypeStruct((s, N_HEADS * HEAD_DIM), BF16),
        scratch_shapes=[pltpu.VMEM((tq, wide), F32), pltpu.VMEM((tq, wide), F32)],
        compiler_params=_params("parallel", "arbitrary"),
        name="stick_breaking",
    )(qk, qk, v)


def _gdn_body(q_ref, k_ref, v_ref, ba_ref, z_ref, cw_ref, alog_ref, dtb_ref, onw_ref, o_ref,
              state_ref, tail_ref, gct_ref, *, c, hb):
    grp = pl.program_id(0)
    step = pl.program_id(1)
    heads = range(hb)

    @pl.when(step == 0)
    def _():
        state_ref[...] = jnp.zeros_like(state_ref)
        tail_ref[...] = jnp.zeros_like(tail_ref)

    row = lax.broadcasted_iota(jnp.int32, (c, c), 0)
    col = lax.broadcasted_iota(jnp.int32, (c, c), 1)
    lane = lax.broadcasted_iota(jnp.int32, (1, LANES), 1)
    lower = row >= col
    eye = (row == col).astype(F32)

    ba = ba_ref[...]
    beta_all = _sigmoid(ba)
    g_all = -jnp.exp(alog_ref[...]) * _softplus(ba + dtb_ref[...])
    gc_all = jnp.dot(lower.astype(F32), g_all, precision=lax.Precision.HIGHEST,
                     preferred_element_type=F32)
    gct_ref[...] = gc_all.T

    def conv_silu(x_ref, sec, s):
        cur = x_ref[:, s * HEAD_DIM:(s + 1) * HEAD_DIM].astype(F32)
        lanes = slice((sec * hb + s) * HEAD_DIM, (sec * hb + s + 1) * HEAD_DIM)
        ext = jnp.concatenate([tail_ref[:, lanes], cur], axis=0)
        w = cw_ref[:, lanes]
        y = cur * w[GDN_CONV - 1:GDN_CONV, :]
        for j in range(GDN_CONV - 1):
            shifted = pltpu.roll(ext, GDN_CONV - 1 - j, axis=0)[SUBLANES:, :]
            y = y + shifted * w[j:j + 1, :]
        tail_ref[:, lanes] = cur[c - SUBLANES:, :]
        return _silu(y)

    def l2norm(x):
        return x * lax.rsqrt(jnp.sum(x * x, axis=-1, keepdims=True) + EPS)

    def column(x, idx):
        picked = jnp.sum(jnp.where(lane == idx, x, 0.0), axis=1, keepdims=True)
        return jnp.broadcast_to(picked, (c, LANES))

    qs = [l2norm(conv_silu(q_ref, 0, s)) * (HEAD_DIM ** -0.5) for s in heads]
    ks = [l2norm(conv_silu(k_ref, 1, s)) for s in heads]
    vs = [conv_silu(v_ref, 2, s) for s in heads]
    betas = [column(beta_all, grp * hb + s) for s in heads]
    gcs = [column(gc_all, N_HEADS + grp * hb + s) for s in heads]
    gc_rows = [gct_ref[pl.ds(N_HEADS + grp * hb + s, 1), :] for s in heads]
    decays = [jnp.where(lower, jnp.exp(jnp.where(lower, gc - gc_row, 0.0)), 0.0)
              for gc, gc_row in zip(gcs, gc_rows)]

    kks = [_bdot_nt(k, k) for k in ks]
    qks = [_bdot_nt(q, k) for q, k in zip(qs, ks)]
    attns = [qk * decay for qk, decay in zip(qks, decays)]
    stricts = [jnp.where(row > col, beta * kk * decay, 0.0)
               for beta, kk, decay in zip(betas, kks, decays)]

    invs = [eye - jnp.where((row >> 1) == (col >> 1), st, 0.0) for st in stricts]
    lvl = 1
    while (1 << lvl) < c:
        in_pair = ((row >> (lvl + 1)) == (col >> (lvl + 1))) & ((row >> lvl) != (col >> lvl))
        offs = [jnp.where(in_pair, st, 0.0).astype(BF16) for st in stricts]
        inv16 = [inv.astype(BF16) for inv in invs]
        left = [jnp.dot(a, b, preferred_element_type=F32) for a, b in zip(inv16, offs)]
        corr = [jnp.dot(a.astype(BF16), b, preferred_element_type=F32) for a, b in zip(left, inv16)]
        invs = [inv - cr for inv, cr in zip(invs, corr)]
        lvl += 1

    exp_gcs = [jnp.exp(gc) for gc in gcs]
    rhss = [jnp.concatenate([beta * v, beta * eg * k], axis=1)
            for beta, v, eg, k in zip(betas, vs, exp_gcs, ks)]
    w_vks = [_bdot(inv, rhs) for inv, rhs in zip(invs, rhss)]

    states = [state_ref[s] for s in heads]
    us = [w_vk[:, :HEAD_DIM] - _bdot(w_vk[:, HEAD_DIM:], st) for w_vk, st in zip(w_vks, states)]
    inter = [_bdot(q * eg, st) for q, eg, st in zip(qs, exp_gcs, states)]
    intra = [_bdot(attn, u) for attn, u in zip(attns, us)]
    gc_lasts = [gc[c - 1:c, :] for gc in gcs]
    k_decs = [k * jnp.exp(gl - gc) for k, gl, gc in zip(ks, gc_lasts, gcs)]
    upd = [_bdot(kd.T, u) for kd, u in zip(k_decs, us)]
    for s in heads:
        state_ref[s] = jnp.exp(gc_lasts[s]) * states[s] + upd[s]
        o = inter[s] + intra[s]
        ms = jnp.mean(o * o, axis=-1, keepdims=True)
        zh = z_ref[:, s * HEAD_DIM:(s + 1) * HEAD_DIM].astype(F32)
        o_ref[:, s * HEAD_DIM:(s + 1) * HEAD_DIM] = (
            o * lax.rsqrt(ms + EPS) * onw_ref[...] * _silu(zh)).astype(o_ref.dtype)


def _gdn(qkv, ba, z, conv_w, alog_row, dtb_row, onw_row, *, c=GDN_CHUNK, hb=GDN_HEADS_PER_STEP):
    s = qkv.shape[0]
    groups = N_HEADS // hb
    wide = hb * HEAD_DIM
    sec = lambda k: pl.BlockSpec((c, wide), lambda g, t, k=k: (t, k * groups + g))
    cw = conv_w.reshape(GDN_CONV, 3, groups, wide).transpose(2, 0, 1, 3).reshape(groups, GDN_CONV, 3 * wide)
    row = pl.BlockSpec((1, LANES), lambda g, t: (0, 0))
    return pl.pallas_call(
        functools.partial(_gdn_body, c=c, hb=hb),
        grid=(groups, s // c),
        in_specs=[sec(0), sec(1), sec(2),
                  pl.BlockSpec((c, LANES), lambda g, t: (t, 0)),
                  pl.BlockSpec((c, wide), lambda g, t: (t, g)),
                  pl.BlockSpec((None, GDN_CONV, 3 * wide), lambda g, t: (g, 0, 0)),
                  row, row, row],
        out_specs=pl.BlockSpec((c, wide), lambda g, t: (t, g)),
        out_shape=jax.ShapeDtypeStruct((s, N_HEADS * HEAD_DIM), BF16),
        scratch_shapes=[pltpu.VMEM((hb, HEAD_DIM, HEAD_DIM), F32),
                        pltpu.VMEM((SUBLANES, 3 * wide), F32),
                        pltpu.VMEM((LANES, c), F32)],
        compiler_params=_params("parallel", "arbitrary"),
        name="gated_deltanet",
    )(qkv, qkv, qkv, ba, z, cw, alog_row, dtb_row, onw_row)


def kernel(x, c, w_mod, b_mod, norm1_w, w_in, q_norm_w, k_norm_w, conv_w, a_log, dt_bias,
           o_norm_w, p_a, p_b, w_out, norm2_w, w_gate, w_up, w_down):
    bsz, s, d = x.shape
    depth = w_mod.shape[0]
    assert bsz == 1, "single-sequence prefill block"
    hd = N_HEADS * HEAD_DIM
    d_ff = w_gate.shape[2]
    h = x[0].astype(F32)
    c_col = c.astype(F32).reshape(d, 1)

    o_v, o_qkvb, o_z = 2 * hd, 3 * hd, 6 * hd
    o_b, o_gates = 7 * hd, 7 * hd + 2 * N_HEADS

    for l in range(depth):
        mod = _adaln(c_col, w_mod[l].astype(F32), b_mod[l].astype(F32).reshape(1, 6 * d))
        shift1, scale1, gate1, shift2, scale2, gate2 = [mod[:, i * d:(i + 1) * d] for i in range(6)]

        u = _norm_mod(h, norm1_w[l].astype(F32).reshape(1, d), scale1, shift1)

        w_t = jnp.transpose(w_in[l])
        tm, tn = 2048, 1024
        qk_norm_w = jnp.concatenate([jnp.tile(q_norm_w[l].astype(F32) * (HEAD_DIM ** -0.5), N_HEADS),
                                     jnp.tile(k_norm_w[l].astype(F32), N_HEADS)]).reshape(1, 2 * hd)
        qk_a = _matmul([(u, w_t, 0, "nk")], [(qk_norm_w, "row", 0)], _ep_head_rmsnorm, 2 * hd, BF16,
                       tm=tm, tn=tn, name="proj_qk")
        v_a = _matmul([(u, w_t, o_v, "nk")], [], _ep_plain, hd, BF16, tm=tm, tn=tn, name="proj_v")
        qkv_b = _matmul([(u, w_t, o_qkvb, "nk")], [], _ep_plain, 3 * hd, BF16, tm=tm, tn=tn,
                        name="proj_qkv_b")
        z_b = _matmul([(u, w_t, o_z, "nk")], [], _ep_plain, hd, BF16, tm=tm, tn=tn, name="proj_z")
        ba = _matmul([(u, w_t, o_b, "nk")], [], _ep_plain, LANES, F32, tm=tm, tn=LANES, name="proj_ba")
        sig_gates = _matmul([(u, w_t, o_gates, "nk")], [], _ep_sigmoid, 2 * d, BF16, tm=tm, tn=tn,
                            name="proj_gates")

        o_a = _sb_attention(qk_a, v_a)

        pad = (0, LANES - 2 * N_HEADS)
        alog_row = jnp.pad(jnp.concatenate([jnp.zeros((N_HEADS,), F32), a_log[l].astype(F32)]), pad)
        dtb_row = jnp.pad(jnp.concatenate([jnp.zeros((N_HEADS,), F32), dt_bias[l].astype(F32)]), pad)
        o_b_out = _gdn(qkv_b, ba, z_b, conv_w[l].astype(F32), alog_row.reshape(1, LANES),
                       dtb_row.reshape(1, LANES), o_norm_w[l].astype(F32).reshape(1, HEAD_DIM))

        tn = 256
        merged = _matmul([(o_a, p_a[l], 0, "kn"), (o_b_out, p_b[l], 0, "kn")],
                         [(sig_gates, "tile", 0), (sig_gates, "tile", d // tn)], _ep_merge, d, BF16,
                         tm=2048, tn=tn, name="merge_proj")
        tn = 512
        h = _matmul([(merged, w_out[l], 0, "kn")], [(h, "tile", 0), (gate1, "row", 0)],
                    _ep_residual, d, F32, tm=2048, tn=tn, name="out_proj")

        u2 = _norm_mod(h, norm2_w[l].astype(F32).reshape(1, d), scale2, shift2)
        ff = _matmul([(u2, w_gate[l], 0, "kn"), (u2, w_up[l], 0, "kn")], [], _ep_swiglu, d_ff, BF16,
                     tm=2048, tn=512, name="ffn_up")
        h = _matmul([(ff, w_down[l], 0, "kn")], [(h, "tile", 0), (gate2, "row", 0)],
                    _ep_residual, d, F32, tm=1024, tn=256, name="ffn_down")
    return h.reshape(bsz, s, d).astype(x.dtype)
```

```python
import functools

import jax
import jax.numpy as jnp
from jax import lax
from jax.experimental import pallas as pl
from jax.experimental.pallas import tpu as pltpu

F32 = jnp.float32
BF16 = jnp.bfloat16

EPS = 1e-6
EXP_UNDERFLOW = 105.0
HEAD_DIM = 128
N_HEADS = 16
GDN_CONV = 4
LANES = 128
SUBLANES = 8
VMEM_LIMIT_BYTES = 56 * 1024 * 1024

MM_ROW_CHUNK = 512
CONV_ROW_CHUNK = 256
SB_K_TILE = 256
SB_Q_TILE = 256
SB_HEADS_PER_STEP = 4
GDN_CHUNK = 128
GDN_HEADS_PER_STEP = 16


def _params(*sem):
    return pltpu.CompilerParams(dimension_semantics=sem, vmem_limit_bytes=VMEM_LIMIT_BYTES)


def _sigmoid(x):
    return 1.0 / (1.0 + jnp.exp(-x))


def _silu(x):
    return x * _sigmoid(x)


def _softplus(x):
    return jnp.maximum(x, 0.0) + jnp.log1p(jnp.exp(-jnp.abs(x)))


def _bdot(a, b):
    return jnp.dot(a.astype(BF16), b.astype(BF16), preferred_element_type=F32)


def _bdot_nt(a, b):
    return lax.dot_general(a.astype(BF16), b.astype(BF16), (((1,), (1,)), ((), ())),
                           preferred_element_type=F32)


def _adaln_body(c_ref, w_ref, b_ref, o_ref):
    c = c_ref[...]
    o_ref[...] = jnp.sum(w_ref[...] * _silu(c), axis=0, keepdims=True) + b_ref[...]


def _adaln(c_col, w, b, *, tn=1024):
    d, n = w.shape
    return pl.pallas_call(
        _adaln_body,
        grid=(n // tn,),
        in_specs=[pl.BlockSpec((d, 1), lambda j: (0, 0)),
                  pl.BlockSpec((d, tn), lambda j: (0, j)),
                  pl.BlockSpec((1, tn), lambda j: (0, j))],
        out_specs=pl.BlockSpec((1, tn), lambda j: (0, j)),
        out_shape=jax.ShapeDtypeStruct((1, n), F32),
        compiler_params=_params("parallel"),
        name="adaln_mod",
    )(c_col, w, b)


def _norm_mod_body(x_ref, w_ref, scale_ref, shift_ref, o_ref):
    x = x_ref[...]
    y = x * lax.rsqrt(jnp.mean(x * x, axis=-1, keepdims=True) + EPS) * w_ref[...]
    o_ref[...] = (y * (1.0 + scale_ref[...]) + shift_ref[...]).astype(o_ref.dtype)


def _norm_mod(x, w, scale, shift, *, tm=512):
    s, d = x.shape
    row = pl.BlockSpec((1, d), lambda i: (0, 0))
    return pl.pallas_call(
        _norm_mod_body,
        grid=(s // tm,),
        in_specs=[pl.BlockSpec((tm, d), lambda i: (i, 0)), row, row, row],
        out_specs=pl.BlockSpec((tm, d), lambda i: (i, 0)),
        out_shape=jax.ShapeDtypeStruct((s, d), BF16),
        compiler_params=_params("parallel"),
        name="norm_mod",
    )(x, w, scale, shift)


def _mm_body(*refs, n_lhs, kinds, n_extra, epilogue, tn):
    pos = n_lhs
    a_refs = [refs[lhs] for _, _, lhs in kinds]
    weights = []
    for layout, shift, _ in kinds:
        if shift:
            w = jnp.concatenate([refs[pos][...], refs[pos + 1][...]], axis=0)[shift:shift + tn, :]
            pos += 2
        else:
            w = refs[pos][...]
            pos += 1
        weights.append(w.astype(BF16))
    extra, o_ref = refs[pos:pos + n_extra], refs[pos + n_extra]
    tm = o_ref.shape[0]
    rows = min(tm, MM_ROW_CHUNK)
    for r in range(tm // rows):
        rs = slice(r * rows, (r + 1) * rows)
        accs = [lax.dot_general(a[rs, :], w, (((1,), (1 if layout == "nk" else 0,)), ((), ())),
                                preferred_element_type=F32)
                for a, w, (layout, _, _) in zip(a_refs, weights, kinds)]
        ex = [e[rs, :] if e.shape[0] == tm else e[...] for e in extra]
        o_ref[rs, :] = epilogue(accs, ex).astype(o_ref.dtype)


def _matmul(pairs, extras, epilogue, n, out_dtype, *, tm, tn, name):
    m = pairs[0][0].shape[0]
    tm = min(tm, m)
    in_specs, args, kinds = [], [], []
    for a, _, _, _ in pairs:
        if not any(a is seen for seen in args):
            in_specs.append(pl.BlockSpec((tm, a.shape[1]), lambda i, j: (i, 0)))
            args.append(a)
    n_lhs = len(args)
    for a, b, col0, layout in pairs:
        shift = col0 % tn
        base = col0 - shift
        if layout == "kn":
            assert shift == 0
            in_specs.append(pl.BlockSpec((b.shape[0], tn), lambda i, j, off=base // tn: (0, j + off)))
            args.append(b)
        else:
            in_specs.append(pl.BlockSpec((tn, b.shape[1]), lambda i, j, off=base // tn: (j + off, 0)))
            args.append(b)
            if shift:
                assert shift % SUBLANES == 0 and base % shift == 0 and tn % shift == 0
                in_specs.append(pl.BlockSpec(
                    (shift, b.shape[1]),
                    lambda i, j, off=base // shift, step=tn // shift: (off + (j + 1) * step, 0)))
                args.append(b)
        kinds.append((layout, shift, [a is lhs for lhs in args[:n_lhs]].index(True)))
    for arr, kind, off in extras:
        if kind == "tile":
            in_specs.append(pl.BlockSpec((tm, tn), lambda i, j, off=off: (i, j + off)))
        else:
            in_specs.append(pl.BlockSpec((1, tn), lambda i, j: (0, j)))
        args.append(arr)
    body = functools.partial(_mm_body, n_lhs=n_lhs, kinds=tuple(kinds), n_extra=len(extras),
                             epilogue=epilogue, tn=tn)
    return pl.pallas_call(
        body,
        grid=(m // tm, n // tn),
        in_specs=in_specs,
        out_specs=pl.BlockSpec((tm, tn), lambda i, j: (i, j)),
        out_shape=jax.ShapeDtypeStruct((m, n), out_dtype),
        compiler_params=_params("parallel", "parallel"),
        name=name,
    )(*args)


def _ep_plain(accs, extra):
    return accs[0]


def _ep_head_rmsnorm(accs, extra):
    acc, w = accs[0], extra[0]
    outs = []
    for g in range(acc.shape[1] // HEAD_DIM):
        blk = acc[:, g * HEAD_DIM:(g + 1) * HEAD_DIM]
        ms = jnp.mean(blk * blk, axis=-1, keepdims=True)
        outs.append(blk * lax.rsqrt(ms + EPS) * w[:, g * HEAD_DIM:(g + 1) * HEAD_DIM])
    return jnp.concatenate(outs, axis=1)


def _ep_sigmoid(accs, extra):
    return _sigmoid(accs[0])


def _ep_silu(accs, extra):
    return _silu(accs[0])


def _ep_merge(accs, extra):
    return extra[0] * accs[0] + extra[1] * accs[1]


def _ep_residual(accs, extra):
    return extra[0] + extra[1] * accs[0]


def _ep_swiglu(accs, extra):
    return _silu(accs[0]) * accs[1]


def _proj_conv_body(a_ref, wt_ref, cw_ref, o_ref, halo_ref, *, rows):
    i, j = pl.program_id(0), pl.program_id(1)

    @pl.when(i == 0)
    def _():
        halo_ref[j] = jnp.zeros(halo_ref.shape[1:], F32)

    w = wt_ref[...].astype(BF16)
    cw = cw_ref[...]
    prev = halo_ref[j]
    for r in range(o_ref.shape[0] // rows):
        rs = slice(r * rows, (r + 1) * rows)
        acc = lax.dot_general(a_ref[rs, :], w, (((1,), (1,)), ((), ())), preferred_element_type=F32)
        ext = jnp.concatenate([prev, acc], axis=0)
        y = acc * cw[GDN_CONV - 1:GDN_CONV, :]
        for t in range(GDN_CONV - 1):
            y = y + pltpu.roll(ext, GDN_CONV - 1 - t, axis=0)[SUBLANES:, :] * cw[t:t + 1, :]
        o_ref[rs, :] = _silu(y).astype(o_ref.dtype)
        prev = acc[rows - SUBLANES:, :]
    halo_ref[j] = prev


def _proj_conv(a, w_t, col0, conv_w, *, tm, tn):
    m, k = a.shape
    n = conv_w.shape[1]
    tm = min(tm, m)
    assert col0 % tn == 0 and n % tn == 0
    return pl.pallas_call(
        functools.partial(_proj_conv_body, rows=min(tm, CONV_ROW_CHUNK)),
        grid=(m // tm, n // tn),
        in_specs=[pl.BlockSpec((tm, k), lambda i, j: (i, 0)),
                  pl.BlockSpec((tn, k), lambda i, j, off=col0 // tn: (j + off, 0)),
                  pl.BlockSpec((GDN_CONV, tn), lambda i, j: (0, j))],
        out_specs=pl.BlockSpec((tm, tn), lambda i, j: (i, j)),
        out_shape=jax.ShapeDtypeStruct((m, n), BF16),
        scratch_shapes=[pltpu.VMEM((n // tn, SUBLANES, tn), F32)],
        compiler_params=_params("arbitrary", "arbitrary"),
        name="proj_qkv_b_conv",
    )(a, w_t, conv_w)


def _sb_body(q_ref, k_ref, v_ref, o_ref, acc_ref, carry_ref, *, tq, tk, hb):
    i = pl.program_id(1)
    ratio = tq // tk
    row = lax.broadcasted_iota(jnp.int32, (tq, tk), 0)
    col = lax.broadcasted_iota(jnp.int32, (tq, tk), 1)
    suffix = (lax.broadcasted_iota(jnp.int32, (tk, tk), 0)
              > lax.broadcasted_iota(jnp.int32, (tk, tk), 1)).astype(BF16)
    heads = [slice(s * HEAD_DIM, (s + 1) * HEAD_DIM) for s in range(hb)]

    def tile(j, causal):
        start = pl.multiple_of(j * tk, tk)
        zs = [lax.dot_general(q_ref[:, h], k_ref[pl.ds(start, tk), h], (((1,), (1,)), ((), ())),
                              preferred_element_type=F32) for h in heads]
        lss, l1ms = [], []
        for z in zs:
            sp = jnp.log(1.0 + jnp.exp(-jnp.abs(z)))
            ls = jnp.minimum(z, 0.0) - sp
            l1m = ls - z
            if causal is not None:
                l1m = jnp.where(causal, l1m, 0.0)
            lss.append(ls)
            l1ms.append(l1m)
        withins = [jnp.dot(l1m.astype(BF16), suffix, preferred_element_type=F32) for l1m in l1ms]
        ws = []
        for h, ls, l1m, within in zip(heads, lss, l1ms, withins):
            carry = carry_ref[:, h]
            w = jnp.exp(ls + within + jnp.concatenate([carry] * (tk // LANES), axis=1))
            if causal is not None:
                w = jnp.where(causal, w, 0.0)
            ws.append(w.astype(BF16))
            carry_ref[:, h] = carry + jnp.sum(l1m, axis=1, keepdims=True)
        for h, w in zip(heads, ws):
            acc_ref[:, h] += jnp.dot(w, v_ref[pl.ds(start, tk), h], preferred_element_type=F32)

    acc_ref[...] = jnp.zeros_like(acc_ref)
    carry_ref[...] = jnp.zeros_like(carry_ref)
    for dj in reversed(range(ratio)):
        tile(i * ratio + dj, col + dj * tk < row)

    def more(st):
        n, live = st
        return jnp.logical_and(n < i * ratio, live)

    def body(st):
        n, _ = st
        tile(i * ratio - 1 - n, None)
        return n + 1, jnp.max(carry_ref[...]) > -EXP_UNDERFLOW

    lax.while_loop(more, body, (jnp.int32(0), jnp.bool_(True)))
    o_ref[...] = acc_ref[...].astype(o_ref.dtype)


def _sb_attention(qk, v, *, tq=SB_Q_TILE, tk=SB_K_TILE, hb=SB_HEADS_PER_STEP):
    s = v.shape[0]
    groups = N_HEADS // hb
    wide = hb * HEAD_DIM
    return pl.pallas_call(
        functools.partial(_sb_body, tq=tq, tk=tk, hb=hb),
        grid=(groups, s // tq),
        in_specs=[pl.BlockSpec((tq, wide), lambda g, i: (i, g)),
                  pl.BlockSpec((s, wide), lambda g, i: (0, groups + g)),
                  pl.BlockSpec((s, wide), lambda g, i: (0, g))],
        out_specs=pl.BlockSpec((tq, wide), lambda g, i: (i, g)),
        out_shape=jax.ShapeDtypeStruct((s, N_HEADS * HEAD_DIM), BF16),
        scratch_shapes=[pltpu.VMEM((tq, wide), F32), pltpu.VMEM((tq, wide), F32)],
        compiler_params=_params("parallel", "arbitrary"),
        name="stick_breaking",
    )(qk, qk, v)


def _gdn_body(q_ref, k_ref, v_ref, ba_ref, z_ref, alog_ref, dtb_ref, onw_ref, o_ref,
              state_ref, gct_ref, *, c, hb):
    grp = pl.program_id(0)
    step = pl.program_id(1)
    heads = range(hb)

    @pl.when(step == 0)
    def _():
        state_ref[...] = jnp.zeros_like(state_ref)

    row = lax.broadcasted_iota(jnp.int32, (c, c), 0)
    col = lax.broadcasted_iota(jnp.int32, (c, c), 1)
    lane = lax.broadcasted_iota(jnp.int32, (1, LANES), 1)
    lower = row >= col
    eye = (row == col).astype(F32)

    ba = ba_ref[...]
    beta_all = _sigmoid(ba)
    g_all = -jnp.exp(alog_ref[...]) * _softplus(ba + dtb_ref[...])
    gc_all = jnp.dot(lower.astype(F32), g_all, precision=lax.Precision.HIGHEST,
                     preferred_element_type=F32)
    gct_ref[...] = gc_all.T

    def head(x_ref, s):
        return x_ref[:, s * HEAD_DIM:(s + 1) * HEAD_DIM].astype(F32)

    def l2norm(x):
        return x * lax.rsqrt(jnp.sum(x * x, axis=-1, keepdims=True) + EPS)

    def column(x, idx):
        picked = jnp.sum(jnp.where(lane == idx, x, 0.0), axis=1, keepdims=True)
        return jnp.broadcast_to(picked, (c, LANES))

    qs = [l2norm(head(q_ref, s)) * (HEAD_DIM ** -0.5) for s in heads]
    ks = [l2norm(head(k_ref, s)) for s in heads]
    vs = [head(v_ref, s) for s in heads]
    betas = [column(beta_all, grp * hb + s) for s in heads]
    gcs = [column(gc_all, N_HEADS + grp * hb + s) for s in heads]
    gc_rows = [gct_ref[pl.ds(N_HEADS + grp * hb + s, 1), :] for s in heads]
    decays = [jnp.where(lower, jnp.exp(jnp.where(lower, gc - gc_row, 0.0)), 0.0)
              for gc, gc_row in zip(gcs, gc_rows)]

    kks = [_bdot_nt(k, k) for k in ks]
    qks = [_bdot_nt(q, k) for q, k in zip(qs, ks)]
    attns = [qk * decay for qk, decay in zip(qks, decays)]
    stricts = [jnp.where(row > col, beta * kk * decay, 0.0)
               for beta, kk, decay in zip(betas, kks, decays)]

    invs = [eye - jnp.where((row >> 1) == (col >> 1), st, 0.0) for st in stricts]
    lvl = 1
    while (1 << lvl) < c:
        in_pair = ((row >> (lvl + 1)) == (col >> (lvl + 1))) & ((row >> lvl) != (col >> lvl))
        offs = [jnp.where(in_pair, st, 0.0).astype(BF16) for st in stricts]
        inv16 = [inv.astype(BF16) for inv in invs]
        left = [jnp.dot(a, b, preferred_element_type=F32) for a, b in zip(inv16, offs)]
        corr = [jnp.dot(a.astype(BF16), b, preferred_element_type=F32) for a, b in zip(left, inv16)]
        invs = [inv - cr for inv, cr in zip(invs, corr)]
        lvl += 1

    exp_gcs = [jnp.exp(gc) for gc in gcs]
    rhss = [jnp.concatenate([beta * v, beta * eg * k], axis=1)
            for beta, v, eg, k in zip(betas, vs, exp_gcs, ks)]
    w_vks = [_bdot(inv, rhs) for inv, rhs in zip(invs, rhss)]

    states = [state_ref[s] for s in heads]
    us = [w_vk[:, :HEAD_DIM] - _bdot(w_vk[:, HEAD_DIM:], st) for w_vk, st in zip(w_vks, states)]
    inter = [_bdot(q * eg, st) for q, eg, st in zip(qs, exp_gcs, states)]
    intra = [_bdot(attn, u) for attn, u in zip(attns, us)]
    gc_lasts = [gc[c - 1:c, :] for gc in gcs]
    k_decs = [k * jnp.exp(gl - gc) for k, gl, gc in zip(ks, gc_lasts, gcs)]
    upd = [_bdot(kd.T, u) for kd, u in zip(k_decs, us)]
    for s in heads:
        state_ref[s] = jnp.exp(gc_lasts[s]) * states[s] + upd[s]
        o = inter[s] + intra[s]
        ms = jnp.mean(o * o, axis=-1, keepdims=True)
        o_ref[:, s * HEAD_DIM:(s + 1) * HEAD_DIM] = (
            o * lax.rsqrt(ms + EPS) * onw_ref[...] * head(z_ref, s)).astype(o_ref.dtype)


def _gdn(qkv, ba, z_gate, alog_row, dtb_row, onw_row, *, c=GDN_CHUNK, hb=GDN_HEADS_PER_STEP):
    s = qkv.shape[0]
    groups = N_HEADS // hb
    wide = hb * HEAD_DIM
    sec = lambda k: pl.BlockSpec((c, wide), lambda g, t, k=k: (t, k * groups + g))
    row = pl.BlockSpec((1, LANES), lambda g, t: (0, 0))
    return pl.pallas_call(
        functools.partial(_gdn_body, c=c, hb=hb),
        grid=(groups, s // c),
        in_specs=[sec(0), sec(1), sec(2),
                  pl.BlockSpec((c, LANES), lambda g, t: (t, 0)),
                  pl.BlockSpec((c, wide), lambda g, t: (t, g)),
                  row, row, row],
        out_specs=pl.BlockSpec((c, wide), lambda g, t: (t, g)),
        out_shape=jax.ShapeDtypeStruct((s, N_HEADS * HEAD_DIM), BF16),
        scratch_shapes=[pltpu.VMEM((hb, HEAD_DIM, HEAD_DIM), F32),
                        pltpu.VMEM((LANES, c), F32)],
        compiler_params=_params("parallel", "arbitrary"),
        name="gated_deltanet",
    )(qkv, qkv, qkv, ba, z_gate, alog_row, dtb_row, onw_row)


def kernel(x, c, w_mod, b_mod, norm1_w, w_in, q_norm_w, k_norm_w, conv_w, a_log, dt_bias,
           o_norm_w, p_a, p_b, w_out, norm2_w, w_gate, w_up, w_down):
    bsz, s, d = x.shape
    depth = w_mod.shape[0]
    assert bsz == 1, "single-sequence prefill block"
    hd = N_HEADS * HEAD_DIM
    d_ff = w_gate.shape[2]
    h = x[0].astype(F32)
    c_col = c.astype(F32).reshape(d, 1)

    o_v, o_qkvb, o_z = 2 * hd, 3 * hd, 6 * hd
    o_b, o_gates = 7 * hd, 7 * hd + 2 * N_HEADS

    for l in range(depth):
        mod = _adaln(c_col, w_mod[l].astype(F32), b_mod[l].astype(F32).reshape(1, 6 * d))
        shift1, scale1, gate1, shift2, scale2, gate2 = [mod[:, i * d:(i + 1) * d] for i in range(6)]

        u = _norm_mod(h, norm1_w[l].astype(F32).reshape(1, d), scale1, shift1)

        w_t = jnp.transpose(w_in[l])
        tm, tn = 2048, 1024
        qk_norm_w = jnp.concatenate([jnp.tile(q_norm_w[l].astype(F32) * (HEAD_DIM ** -0.5), N_HEADS),
                                     jnp.tile(k_norm_w[l].astype(F32), N_HEADS)]).reshape(1, 2 * hd)
        qk_a = _matmul([(u, w_t, 0, "nk")], [(qk_norm_w, "row", 0)], _ep_head_rmsnorm, 2 * hd, BF16,
                       tm=tm, tn=tn, name="proj_qk")
        v_a = _matmul([(u, w_t, o_v, "nk")], [], _ep_plain, hd, BF16, tm=tm, tn=tn, name="proj_v")
        qkv_b = _proj_conv(u, w_t, o_qkvb, conv_w[l].astype(F32), tm=tm, tn=tn)
        z_gate = _matmul([(u, w_t, o_z, "nk")], [], _ep_silu, hd, BF16, tm=tm, tn=tn, name="proj_z")
        ba = _matmul([(u, w_t, o_b, "nk")], [], _ep_plain, LANES, F32, tm=tm, tn=LANES, name="proj_ba")
        sig_gates = _matmul([(u, w_t, o_gates, "nk")], [], _ep_sigmoid, 2 * d, BF16, tm=tm, tn=tn,
                            name="proj_gates")

        o_a = _sb_attention(qk_a, v_a)

        pad = (0, LANES - 2 * N_HEADS)
        alog_row = jnp.pad(jnp.concatenate([jnp.zeros((N_HEADS,), F32), a_log[l].astype(F32)]), pad)
        dtb_row = jnp.pad(jnp.concatenate([jnp.zeros((N_HEADS,), F32), dt_bias[l].astype(F32)]), pad)
        o_b_out = _gdn(qkv_b, ba, z_gate, alog_row.reshape(1, LANES), dtb_row.reshape(1, LANES),
                       o_norm_w[l].astype(F32).reshape(1, HEAD_DIM))

        tn = 256
        merged = _matmul([(o_a, p_a[l], 0, "kn"), (o_b_out, p_b[l], 0, "kn")],
                         [(sig_gates, "tile", 0), (sig_gates, "tile", d // tn)], _ep_merge, d, BF16,
                         tm=2048, tn=tn, name="merge_proj")
        tn = 512
        h = _matmul([(merged, w_out[l], 0, "kn")], [(h, "tile", 0), (gate1, "row", 0)],
                    _ep_residual, d, F32, tm=2048, tn=tn, name="out_proj")

        u2 = _norm_mod(h, norm2_w[l].astype(F32).reshape(1, d), scale2, shift2)
        ff = _matmul([(u2, w_gate[l], 0, "kn"), (u2, w_up[l], 0, "kn")], [], _ep_swiglu, d_ff, BF16,
                     tm=2048, tn=512, name="ffn_up")
        h = _matmul([(ff, w_down[l], 0, "kn")], [(h, "tile", 0), (gate2, "row", 0)],
                    _ep_residual, d, F32, tm=1024, tn=256, name="ffn_down")
    return h.reshape(bsz, s, d).astype(x.dtype)
```

```python
import functools

import jax
import jax.numpy as jnp
from jax import lax
from jax.experimental import pallas as pl
from jax.experimental.pallas import tpu as pltpu

F32 = jnp.float32
BF16 = jnp.bfloat16

EPS = 1e-6
LOG2E = 1.4426950408889634
EXP_UNDERFLOW = 105.0
HEAD_DIM = 128
N_HEADS = 16
GDN_CONV = 4
LANES = 128
SUBLANES = 8
VMEM_LIMIT_BYTES = 56 * 1024 * 1024

TILES = {
    "in_proj": (2048, 1024),
    "merge": (2048, 256),
    "out_proj": (2048, 512),
    "ffn_up": (2048, 512),
    "ffn_down": (1024, 256),
}
MM_ROW_CHUNK = 512
CONV_ROW_CHUNK = 256
SB_K_TILE = 256
SB_Q_TILE = 256
SB_HEADS_PER_STEP = 4
GDN_CHUNK = 128
GDN_HEADS_PER_STEP = 16


def _params(*sem):
    return pltpu.CompilerParams(dimension_semantics=sem, vmem_limit_bytes=VMEM_LIMIT_BYTES)


def _sigmoid(x):
    return 1.0 / (1.0 + jnp.exp(-x))


def _silu(x):
    return x * _sigmoid(x)


def _softplus(x):
    return jnp.maximum(x, 0.0) + jnp.log1p(jnp.exp(-jnp.abs(x)))


def _bdot(a, b):
    return jnp.dot(a.astype(BF16), b.astype(BF16), preferred_element_type=F32)


def _bdot_nt(a, b):
    return lax.dot_general(a.astype(BF16), b.astype(BF16), (((1,), (1,)), ((), ())),
                           preferred_element_type=F32)


def _adaln_body(c_ref, w_ref, b_ref, o_ref):
    c = c_ref[...]
    o_ref[...] = jnp.sum(w_ref[...] * _silu(c), axis=0, keepdims=True) + b_ref[...]


def _adaln(c_col, w, b, *, tn=1024):
    d, n = w.shape
    return pl.pallas_call(
        _adaln_body,
        grid=(n // tn,),
        in_specs=[pl.BlockSpec((d, 1), lambda j: (0, 0)),
                  pl.BlockSpec((d, tn), lambda j: (0, j)),
                  pl.BlockSpec((1, tn), lambda j: (0, j))],
        out_specs=pl.BlockSpec((1, tn), lambda j: (0, j)),
        out_shape=jax.ShapeDtypeStruct((1, n), F32),
        compiler_params=_params("parallel"),
        name="adaln_mod",
    )(c_col, w, b)


def _norm_mod_body(x_ref, w_ref, scale_ref, shift_ref, o_ref):
    x = x_ref[...]
    y = x * lax.rsqrt(jnp.mean(x * x, axis=-1, keepdims=True) + EPS) * w_ref[...]
    o_ref[...] = (y * (1.0 + scale_ref[...]) + shift_ref[...]).astype(o_ref.dtype)


def _norm_mod(x, w, scale, shift, *, tm=1024):
    s, d = x.shape
    row = pl.BlockSpec((1, d), lambda i: (0, 0))
    return pl.pallas_call(
        _norm_mod_body,
        grid=(s // tm,),
        in_specs=[pl.BlockSpec((tm, d), lambda i: (i, 0)), row, row, row],
        out_specs=pl.BlockSpec((tm, d), lambda i: (i, 0)),
        out_shape=jax.ShapeDtypeStruct((s, d), BF16),
        compiler_params=_params("parallel"),
        name="norm_mod",
    )(x, w, scale, shift)


def _mm_body(*refs, n_lhs, kinds, n_extra, epilogue, tn):
    pos = n_lhs
    a_refs = [refs[lhs] for _, _, lhs in kinds]
    weights = []
    for layout, shift, _ in kinds:
        if shift:
            w = jnp.concatenate([refs[pos][...], refs[pos + 1][...]], axis=0)[shift:shift + tn, :]
            pos += 2
        else:
            w = refs[pos][...]
            pos += 1
        weights.append(w.astype(BF16))
    extra, o_ref = refs[pos:pos + n_extra], refs[pos + n_extra]
    tm = o_ref.shape[0]
    rows = min(tm, MM_ROW_CHUNK)
    for r in range(tm // rows):
        rs = slice(r * rows, (r + 1) * rows)
        accs = [lax.dot_general(a[rs, :], w, (((1,), (1 if layout == "nk" else 0,)), ((), ())),
                                preferred_element_type=F32)
                for a, w, (layout, _, _) in zip(a_refs, weights, kinds)]
        ex = [e[rs, :] if e.shape[0] == tm else e[...] for e in extra]
        o_ref[rs, :] = epilogue(accs, ex).astype(o_ref.dtype)


def _matmul(pairs, extras, epilogue, n, out_dtype, *, tm, tn, name):
    m = pairs[0][0].shape[0]
    tm = min(tm, m)
    in_specs, args, kinds = [], [], []
    for a, _, _, _ in pairs:
        if not any(a is seen for seen in args):
            in_specs.append(pl.BlockSpec((tm, a.shape[1]), lambda i, j: (i, 0)))
            args.append(a)
    n_lhs = len(args)
    for a, b, col0, layout in pairs:
        shift = col0 % tn
        base = col0 - shift
        if layout == "kn":
            assert shift == 0
            in_specs.append(pl.BlockSpec((b.shape[0], tn), lambda i, j, off=base // tn: (0, j + off)))
            args.append(b)
        else:
            in_specs.append(pl.BlockSpec((tn, b.shape[1]), lambda i, j, off=base // tn: (j + off, 0)))
            args.append(b)
            if shift:
                assert shift % SUBLANES == 0 and base % shift == 0 and tn % shift == 0
                in_specs.append(pl.BlockSpec(
                    (shift, b.shape[1]),
                    lambda i, j, off=base // shift, step=tn // shift: (off + (j + 1) * step, 0)))
                args.append(b)
        kinds.append((layout, shift, [a is lhs for lhs in args[:n_lhs]].index(True)))
    for arr, kind, off in extras:
        if kind == "tile":
            in_specs.append(pl.BlockSpec((tm, tn), lambda i, j, off=off: (i, j + off)))
        else:
            in_specs.append(pl.BlockSpec((1, tn), lambda i, j: (0, j)))
        args.append(arr)
    body = functools.partial(_mm_body, n_lhs=n_lhs, kinds=tuple(kinds), n_extra=len(extras),
                             epilogue=epilogue, tn=tn)
    return pl.pallas_call(
        body,
        grid=(m // tm, n // tn),
        in_specs=in_specs,
        out_specs=pl.BlockSpec((tm, tn), lambda i, j: (i, j)),
        out_shape=jax.ShapeDtypeStruct((m, n), out_dtype),
        compiler_params=_params("parallel", "parallel"),
        name=name,
    )(*args)


def _ep_plain(accs, extra):
    return accs[0]


def _ep_head_rmsnorm(accs, extra):
    acc, w = accs[0], extra[0]
    outs = []
    for g in range(acc.shape[1] // HEAD_DIM):
        blk = acc[:, g * HEAD_DIM:(g + 1) * HEAD_DIM]
        ms = jnp.mean(blk * blk, axis=-1, keepdims=True)
        outs.append(blk * lax.rsqrt(ms + EPS) * w[:, g * HEAD_DIM:(g + 1) * HEAD_DIM])
    return jnp.concatenate(outs, axis=1)


def _ep_sigmoid(accs, extra):
    return _sigmoid(accs[0])


def _ep_silu(accs, extra):
    return _silu(accs[0])


def _ep_merge(accs, extra):
    return extra[0] * accs[0] + extra[1] * accs[1]


def _ep_residual(accs, extra):
    return extra[0] + extra[1] * accs[0]


def _ep_swiglu(accs, extra):
    return _silu(accs[0]) * accs[1]


def _proj_conv_body(a_ref, wt_ref, cw_ref, o_ref, halo_ref, *, rows):
    i, j = pl.program_id(0), pl.program_id(1)

    @pl.when(i == 0)
    def _():
        halo_ref[j] = jnp.zeros(halo_ref.shape[1:], F32)

    w = wt_ref[...].astype(BF16)
    cw = cw_ref[...]
    prev = halo_ref[j]
    for r in range(o_ref.shape[0] // rows):
        rs = slice(r * rows, (r + 1) * rows)
        acc = lax.dot_general(a_ref[rs, :], w, (((1,), (1,)), ((), ())), preferred_element_type=F32)
        ext = jnp.concatenate([prev, acc], axis=0)
        back1 = pltpu.roll(ext, 1, axis=0)
        older = pltpu.roll(ext * cw[1:2, :] + back1 * cw[0:1, :], 2, axis=0)
        y = acc * cw[3:4, :] + back1[SUBLANES:, :] * cw[2:3, :] + older[SUBLANES:, :]
        o_ref[rs, :] = _silu(y).astype(o_ref.dtype)
        prev = acc[rows - SUBLANES:, :]
    halo_ref[j] = prev


def _proj_conv(a, w_t, col0, conv_w, *, tm, tn):
    m, k = a.shape
    n = conv_w.shape[1]
    tm = min(tm, m)
    assert col0 % tn == 0 and n % tn == 0
    assert conv_w.shape[0] == GDN_CONV == 4, "the kernel body spells out four taps"
    return pl.pallas_call(
        functools.partial(_proj_conv_body, rows=min(tm, CONV_ROW_CHUNK)),
        grid=(m // tm, n // tn),
        in_specs=[pl.BlockSpec((tm, k), lambda i, j: (i, 0)),
                  pl.BlockSpec((tn, k), lambda i, j, off=col0 // tn: (j + off, 0)),
                  pl.BlockSpec((GDN_CONV, tn), lambda i, j: (0, j))],
        out_specs=pl.BlockSpec((tm, tn), lambda i, j: (i, j)),
        out_shape=jax.ShapeDtypeStruct((m, n), BF16),
        scratch_shapes=[pltpu.VMEM((n // tn, SUBLANES, tn), F32)],
        compiler_params=_params("arbitrary", "arbitrary"),
        name="proj_qkv_b_conv",
    )(a, w_t, conv_w)


def _sb_body(q_ref, k_ref, v_ref, o_ref, acc_ref, carry_ref, *, tq, tk, hb):
    i = pl.program_id(1)
    ratio = tq // tk
    row = lax.broadcasted_iota(jnp.int32, (tq, tk), 0)
    col = lax.broadcasted_iota(jnp.int32, (tq, tk), 1)
    suffix = (lax.broadcasted_iota(jnp.int32, (tk, tk), 0)
              > lax.broadcasted_iota(jnp.int32, (tk, tk), 1)).astype(BF16)
    heads = [slice(s * HEAD_DIM, (s + 1) * HEAD_DIM) for s in range(hb)]

    def tile(j, causal):
        start = pl.multiple_of(j * tk, tk)
        zs = [lax.dot_general(q_ref[:, h], k_ref[pl.ds(start, tk), h], (((1,), (1,)), ((), ())),
                              preferred_element_type=F32) for h in heads]
        lss, l1ms = [], []
        for z in zs:
            sp = jnp.log(1.0 + jnp.exp2(jnp.abs(z) * (-LOG2E)))
            ls = jnp.minimum(z, 0.0) - sp
            l1m = ls - z
            if causal is not None:
                l1m = jnp.where(causal, l1m, 0.0)
            lss.append(ls)
            l1ms.append(l1m)
        withins = [jnp.dot(l1m.astype(BF16), suffix, preferred_element_type=F32) for l1m in l1ms]
        ws = []
        for h, ls, l1m, within in zip(heads, lss, l1ms, withins):
            carry = carry_ref[:, h]
            w = jnp.exp(ls + within + jnp.concatenate([carry] * (tk // LANES), axis=1))
            if causal is not None:
                w = jnp.where(causal, w, 0.0)
            ws.append(w.astype(BF16))
            carry_ref[:, h] = carry + jnp.sum(l1m, axis=1, keepdims=True)
        for h, w in zip(heads, ws):
            acc_ref[:, h] += jnp.dot(w, v_ref[pl.ds(start, tk), h], preferred_element_type=F32)

    acc_ref[...] = jnp.zeros_like(acc_ref)
    carry_ref[...] = jnp.zeros_like(carry_ref)
    for dj in reversed(range(ratio)):
        tile(i * ratio + dj, col + dj * tk < row)

    def more(st):
        n, live = st
        return jnp.logical_and(n < i * ratio, live)

    def body(st):
        n, _ = st
        tile(i * ratio - 1 - n, None)
        return n + 1, jnp.max(carry_ref[...]) > -EXP_UNDERFLOW

    lax.while_loop(more, body, (jnp.int32(0), jnp.bool_(True)))
    o_ref[...] = acc_ref[...].astype(o_ref.dtype)


def _sb_attention(qk, v, *, tq=SB_Q_TILE, tk=SB_K_TILE, hb=SB_HEADS_PER_STEP):
    s = v.shape[0]
    groups = N_HEADS // hb
    wide = hb * HEAD_DIM
    return pl.pallas_call(
        functools.partial(_sb_body, tq=tq, tk=tk, hb=hb),
        grid=(groups, s // tq),
        in_specs=[pl.BlockSpec((tq, wide), lambda g, i: (i, g)),
                  pl.BlockSpec((s, wide), lambda g, i: (0, groups + g)),
                  pl.BlockSpec((s, wide), lambda g, i: (0, g))],
        out_specs=pl.BlockSpec((tq, wide), lambda g, i: (i, g)),
        out_shape=jax.ShapeDtypeStruct((s, N_HEADS * HEAD_DIM), BF16),
        scratch_shapes=[pltpu.VMEM((tq, wide), F32), pltpu.VMEM((tq, wide), F32)],
        compiler_params=_params("parallel", "arbitrary"),
        name="stick_breaking",
    )(qk, qk, v)


def _gdn_body(q_ref, k_ref, v_ref, ba_ref, z_ref, alog_ref, dtb_ref, onw_ref, o_ref,
              state_ref, gct_ref, *, c, hb):
    grp = pl.program_id(0)
    step = pl.program_id(1)
    heads = range(hb)

    @pl.when(step == 0)
    def _():
        state_ref[...] = jnp.zeros_like(state_ref)

    row = lax.broadcasted_iota(jnp.int32, (c, c), 0)
    col = lax.broadcasted_iota(jnp.int32, (c, c), 1)
    lane = lax.broadcasted_iota(jnp.int32, (1, LANES), 1)
    lower = row >= col
    eye = (row == col).astype(F32)

    ba = ba_ref[...]
    beta_all = _sigmoid(ba)
    g_all = -jnp.exp(alog_ref[...]) * _softplus(ba + dtb_ref[...])
    gc_all = jnp.dot(lower.astype(F32), g_all, precision=lax.Precision.HIGHEST,
                     preferred_element_type=F32)
    gct_ref[...] = gc_all.T

    def head(x_ref, s):
        return x_ref[:, s * HEAD_DIM:(s + 1) * HEAD_DIM].astype(F32)

    def l2norm(x):
        return x * lax.rsqrt(jnp.sum(x * x, axis=-1, keepdims=True) + EPS)

    def column(x, idx):
        picked = jnp.sum(jnp.where(lane == idx, x, 0.0), axis=1, keepdims=True)
        return jnp.broadcast_to(picked, (c, LANES))

    qs = [l2norm(head(q_ref, s)) * (HEAD_DIM ** -0.5) for s in heads]
    ks = [l2norm(head(k_ref, s)) for s in heads]
    vs = [head(v_ref, s) for s in heads]
    betas = [column(beta_all, grp * hb + s) for s in heads]
    gcs = [column(gc_all, N_HEADS + grp * hb + s) for s in heads]
    gc_rows = [gct_ref[pl.ds(N_HEADS + grp * hb + s, 1), :] for s in heads]
    decays = [jnp.where(lower, jnp.exp(jnp.where(lower, gc - gc_row, 0.0)), 0.0)
              for gc, gc_row in zip(gcs, gc_rows)]

    kks = [_bdot_nt(k, k) for k in ks]
    qks = [_bdot_nt(q, k) for q, k in zip(qs, ks)]
    attns = [qk * decay for qk, decay in zip(qks, decays)]
    stricts = [jnp.where(row > col, beta * kk * decay, 0.0)
               for beta, kk, decay in zip(betas, kks, decays)]

    invs = [eye - jnp.where((row >> 1) == (col >> 1), st, 0.0) for st in stricts]
    lvl = 1
    while (1 << lvl) < c:
        in_pair = ((row >> (lvl + 1)) == (col >> (lvl + 1))) & ((row >> lvl) != (col >> lvl))
        offs = [jnp.where(in_pair, st, 0.0).astype(BF16) for st in stricts]
        inv16 = [inv.astype(BF16) for inv in invs]
        left = [jnp.dot(a, b, preferred_element_type=F32) for a, b in zip(inv16, offs)]
        corr = [jnp.dot(a.astype(BF16), b, preferred_element_type=F32) for a, b in zip(left, inv16)]
        invs = [inv - cr for inv, cr in zip(invs, corr)]
        lvl += 1

    exp_gcs = [jnp.exp(gc) for gc in gcs]
    rhss = [jnp.concatenate([beta * v, beta * eg * k], axis=1)
            for beta, v, eg, k in zip(betas, vs, exp_gcs, ks)]
    w_vks = [_bdot(inv, rhs) for inv, rhs in zip(invs, rhss)]

    states = [state_ref[s] for s in heads]
    us = [w_vk[:, :HEAD_DIM] - _bdot(w_vk[:, HEAD_DIM:], st) for w_vk, st in zip(w_vks, states)]
    inter = [_bdot(q * eg, st) for q, eg, st in zip(qs, exp_gcs, states)]
    intra = [_bdot(attn, u) for attn, u in zip(attns, us)]
    gc_lasts = [gc[c - 1:c, :] for gc in gcs]
    k_decs = [k * jnp.exp(gl - gc) for k, gl, gc in zip(ks, gc_lasts, gcs)]
    upd = [_bdot(kd.T, u) for kd, u in zip(k_decs, us)]
    for s in heads:
        state_ref[s] = jnp.exp(gc_lasts[s]) * states[s] + upd[s]
        o = inter[s] + intra[s]
        ms = jnp.mean(o * o, axis=-1, keepdims=True)
        o_ref[:, s * HEAD_DIM:(s + 1) * HEAD_DIM] = (
            o * lax.rsqrt(ms + EPS) * onw_ref[...] * head(z_ref, s)).astype(o_ref.dtype)


def _gdn(qkv, ba, z_gate, alog_row, dtb_row, onw_row, *, c=GDN_CHUNK, hb=GDN_HEADS_PER_STEP):
    s = qkv.shape[0]
    groups = N_HEADS // hb
    wide = hb * HEAD_DIM
    sec = lambda k: pl.BlockSpec((c, wide), lambda g, t, k=k: (t, k * groups + g))
    row = pl.BlockSpec((1, LANES), lambda g, t: (0, 0))
    return pl.pallas_call(
        functools.partial(_gdn_body, c=c, hb=hb),
        grid=(groups, s // c),
        in_specs=[sec(0), sec(1), sec(2),
                  pl.BlockSpec((c, LANES), lambda g, t: (t, 0)),
                  pl.BlockSpec((c, wide), lambda g, t: (t, g)),
                  row, row, row],
        out_specs=pl.BlockSpec((c, wide), lambda g, t: (t, g)),
        out_shape=jax.ShapeDtypeStruct((s, N_HEADS * HEAD_DIM), BF16),
        scratch_shapes=[pltpu.VMEM((hb, HEAD_DIM, HEAD_DIM), F32),
                        pltpu.VMEM((LANES, c), F32)],
        compiler_params=_params("parallel", "arbitrary"),
        name="gated_deltanet",
    )(qkv, qkv, qkv, ba, z_gate, alog_row, dtb_row, onw_row)


def kernel(x, c, w_mod, b_mod, norm1_w, w_in, q_norm_w, k_norm_w, conv_w, a_log, dt_bias,
           o_norm_w, p_a, p_b, w_out, norm2_w, w_gate, w_up, w_down):
    bsz, s, d = x.shape
    depth = w_mod.shape[0]
    assert bsz == 1, "single-sequence prefill block"
    hd = N_HEADS * HEAD_DIM
    d_ff = w_gate.shape[2]
    h = x[0].astype(F32)
    c_col = c.astype(F32).reshape(d, 1)

    o_v, o_qkvb, o_z = 2 * hd, 3 * hd, 6 * hd
    o_b, o_gates = 7 * hd, 7 * hd + 2 * N_HEADS

    for l in range(depth):
        mod = _adaln(c_col, w_mod[l].astype(F32), b_mod[l].astype(F32).reshape(1, 6 * d))
        shift1, scale1, gate1, shift2, scale2, gate2 = [mod[:, i * d:(i + 1) * d] for i in range(6)]

        u = _norm_mod(h, norm1_w[l].astype(F32).reshape(1, d), scale1, shift1)

        w_t = jnp.transpose(w_in[l])
        tm, tn = TILES["in_proj"]
        qk_norm_w = jnp.concatenate([jnp.tile(q_norm_w[l].astype(F32) * (HEAD_DIM ** -0.5), N_HEADS),
                                     jnp.tile(k_norm_w[l].astype(F32), N_HEADS)]).reshape(1, 2 * hd)
        qk_a = _matmul([(u, w_t, 0, "nk")], [(qk_norm_w, "row", 0)], _ep_head_rmsnorm, 2 * hd, BF16,
                       tm=tm, tn=tn, name="proj_qk")
        v_a = _matmul([(u, w_t, o_v, "nk")], [], _ep_plain, hd, BF16, tm=tm, tn=tn, name="proj_v")
        qkv_b = _proj_conv(u, w_t, o_qkvb, conv_w[l].astype(F32), tm=tm, tn=tn)
        z_gate = _matmul([(u, w_t, o_z, "nk")], [], _ep_silu, hd, BF16, tm=tm, tn=tn, name="proj_z")
        ba = _matmul([(u, w_t, o_b, "nk")], [], _ep_plain, LANES, F32, tm=tm, tn=LANES, name="proj_ba")
        sig_gates = _matmul([(u, w_t, o_gates, "nk")], [], _ep_sigmoid, 2 * d, BF16, tm=tm, tn=tn,
                            name="proj_gates")

        o_a = _sb_attention(qk_a, v_a)

        pad = (0, LANES - 2 * N_HEADS)
        alog_row = jnp.pad(jnp.concatenate([jnp.zeros((N_HEADS,), F32), a_log[l].astype(F32)]), pad)
        dtb_row = jnp.pad(jnp.concatenate([jnp.zeros((N_HEADS,), F32), dt_bias[l].astype(F32)]), pad)
        o_b_out = _gdn(qkv_b, ba, z_gate, alog_row.reshape(1, LANES), dtb_row.reshape(1, LANES),
                       o_norm_w[l].astype(F32).reshape(1, HEAD_DIM))

        tm, tn = TILES["merge"]
        merged = _matmul([(o_a, p_a[l], 0, "kn"), (o_b_out, p_b[l], 0, "kn")],
                         [(sig_gates, "tile", 0), (sig_gates, "tile", d // tn)], _ep_merge, d, BF16,
                         tm=tm, tn=tn, name="merge_proj")
        tm, tn = TILES["out_proj"]
        h = _matmul([(merged, w_out[l], 0, "kn")], [(h, "tile", 0), (gate1, "row", 0)],
                    _ep_residual, d, F32, tm=tm, tn=tn, name="out_proj")

        u2 = _norm_mod(h, norm2_w[l].astype(F32).reshape(1, d), scale2, shift2)
        tm, tn = TILES["ffn_up"]
        ff = _matmul([(u2, w_gate[l], 0, "kn"), (u2, w_up[l], 0, "kn")], [], _ep_swiglu, d_ff, BF16,
                     tm=tm, tn=tn, name="ffn_up")
        tm, tn = TILES["ffn_down"]
        h = _matmul([(ff, w_down[l], 0, "kn")], [(h, "tile", 0), (gate2, "row", 0)],
                    _ep_residual, d, F32, tm=tm, tn=tn, name="ffn_down")
    return h.reshape(bsz, s, d).astype(x.dtype)
```

```python
import functools

import jax
import jax.numpy as jnp
from jax import lax
from jax.experimental import pallas as pl
from jax.experimental.pallas import tpu as pltpu

F32 = jnp.float32
BF16 = jnp.bfloat16

EPS = 1e-6
LOG2E = 1.4426950408889634
EXP_UNDERFLOW = 105.0
HEAD_DIM = 128
N_HEADS = 16
GDN_CONV = 4
LANES = 128
SUBLANES = 8
VMEM_LIMIT_BYTES = 56 * 1024 * 1024

TILES = {
    "in_proj": (2048, 1024),
    "merge": (2048, 256),
    "out_proj": (2048, 512),
    "ffn_up": (2048, 512),
    "ffn_down": (1024, 512),
}
MM_ROW_CHUNK = 256
CONV_ROW_CHUNK = 256
SB_K_TILE = 256
SB_Q_TILE = 256
SB_HEADS_PER_STEP = 4
GDN_CHUNK = 128
GDN_HEADS_PER_STEP = 16


def _params(*sem):
    return pltpu.CompilerParams(dimension_semantics=sem, vmem_limit_bytes=VMEM_LIMIT_BYTES)


def _sigmoid(x):
    return 1.0 / (1.0 + jnp.exp(-x))


def _silu(x):
    return x * _sigmoid(x)


def _softplus(x):
    return jnp.maximum(x, 0.0) + jnp.log1p(jnp.exp(-jnp.abs(x)))


def _bdot(a, b):
    return jnp.dot(a.astype(BF16), b.astype(BF16), preferred_element_type=F32)


def _bdot_nt(a, b):
    return lax.dot_general(a.astype(BF16), b.astype(BF16), (((1,), (1,)), ((), ())),
                           preferred_element_type=F32)


def _adaln_body(c_ref, w_ref, b_ref, o_ref):
    c = c_ref[...]
    o_ref[...] = jnp.sum(w_ref[...] * _silu(c), axis=0, keepdims=True) + b_ref[...]


def _adaln(c_col, w, b, *, tn=1024):
    d, n = w.shape
    return pl.pallas_call(
        _adaln_body,
        grid=(n // tn,),
        in_specs=[pl.BlockSpec((d, 1), lambda j: (0, 0)),
                  pl.BlockSpec((d, tn), lambda j: (0, j)),
                  pl.BlockSpec((1, tn), lambda j: (0, j))],
        out_specs=pl.BlockSpec((1, tn), lambda j: (0, j)),
        out_shape=jax.ShapeDtypeStruct((1, n), F32),
        compiler_params=_params("parallel"),
        name="adaln_mod",
    )(c_col, w, b)


def _norm_mod_body(x_ref, w_ref, scale_ref, shift_ref, o_ref):
    x = x_ref[...]
    y = x * lax.rsqrt(jnp.mean(x * x, axis=-1, keepdims=True) + EPS) * w_ref[...]
    o_ref[...] = (y * (1.0 + scale_ref[...]) + shift_ref[...]).astype(o_ref.dtype)


def _norm_mod(x, w, scale, shift, *, tm=1024):
    s, d = x.shape
    row = pl.BlockSpec((1, d), lambda i: (0, 0))
    return pl.pallas_call(
        _norm_mod_body,
        grid=(s // tm,),
        in_specs=[pl.BlockSpec((tm, d), lambda i: (i, 0)), row, row, row],
        out_specs=pl.BlockSpec((tm, d), lambda i: (i, 0)),
        out_shape=jax.ShapeDtypeStruct((s, d), BF16),
        compiler_params=_params("parallel"),
        name="norm_mod",
    )(x, w, scale, shift)


def _mm_body(*refs, n_lhs, kinds, n_extra, epilogue, tn):
    pos = n_lhs
    a_refs = [refs[lhs] for _, _, lhs in kinds]
    weights = []
    for layout, shift, _ in kinds:
        if shift:
            w = jnp.concatenate([refs[pos][...], refs[pos + 1][...]], axis=0)[shift:shift + tn, :]
            pos += 2
        else:
            w = refs[pos][...]
            pos += 1
        weights.append(w.astype(BF16))
    extra, o_ref = refs[pos:pos + n_extra], refs[pos + n_extra]
    tm = o_ref.shape[0]
    rows = min(tm, MM_ROW_CHUNK)
    for r in range(tm // rows):
        rs = slice(r * rows, (r + 1) * rows)
        accs = [lax.dot_general(a[rs, :], w, (((1,), (1 if layout == "nk" else 0,)), ((), ())),
                                preferred_element_type=F32)
                for a, w, (layout, _, _) in zip(a_refs, weights, kinds)]
        ex = [e[rs, :] if e.shape[0] == tm else e[...] for e in extra]
        o_ref[rs, :] = epilogue(accs, ex).astype(o_ref.dtype)


def _matmul(pairs, extras, epilogue, n, out_dtype, *, tm, tn, name):
    m = pairs[0][0].shape[0]
    tm = min(tm, m)
    in_specs, args, kinds = [], [], []
    for a, _, _, _ in pairs:
        if not any(a is seen for seen in args):
            in_specs.append(pl.BlockSpec((tm, a.shape[1]), lambda i, j: (i, 0)))
            args.append(a)
    n_lhs = len(args)
    for a, b, col0, layout in pairs:
        shift = col0 % tn
        base = col0 - shift
        if layout == "kn":
            assert shift == 0
            in_specs.append(pl.BlockSpec((b.shape[0], tn), lambda i, j, off=base // tn: (0, j + off)))
            args.append(b)
        else:
            in_specs.append(pl.BlockSpec((tn, b.shape[1]), lambda i, j, off=base // tn: (j + off, 0)))
            args.append(b)
            if shift:
                assert shift % SUBLANES == 0 and base % shift == 0 and tn % shift == 0
                in_specs.append(pl.BlockSpec(
                    (shift, b.shape[1]),
                    lambda i, j, off=base // shift, step=tn // shift: (off + (j + 1) * step, 0)))
                args.append(b)
        kinds.append((layout, shift, [a is lhs for lhs in args[:n_lhs]].index(True)))
    for arr, kind, off in extras:
        if kind == "tile":
            in_specs.append(pl.BlockSpec((tm, tn), lambda i, j, off=off: (i, j + off)))
        else:
            in_specs.append(pl.BlockSpec((1, tn), lambda i, j: (0, j)))
        args.append(arr)
    body = functools.partial(_mm_body, n_lhs=n_lhs, kinds=tuple(kinds), n_extra=len(extras),
                             epilogue=epilogue, tn=tn)
    return pl.pallas_call(
        body,
        grid=(m // tm, n // tn),
        in_specs=in_specs,
        out_specs=pl.BlockSpec((tm, tn), lambda i, j: (i, j)),
        out_shape=jax.ShapeDtypeStruct((m, n), out_dtype),
        compiler_params=_params("parallel", "parallel"),
        name=name,
    )(*args)


def _ep_plain(accs, extra):
    return accs[0]


def _ep_head_rmsnorm(accs, extra):
    acc, w = accs[0], extra[0]
    outs = []
    for g in range(acc.shape[1] // HEAD_DIM):
        blk = acc[:, g * HEAD_DIM:(g + 1) * HEAD_DIM]
        ms = jnp.mean(blk * blk, axis=-1, keepdims=True)
        outs.append(blk * lax.rsqrt(ms + EPS) * w[:, g * HEAD_DIM:(g + 1) * HEAD_DIM])
    return jnp.concatenate(outs, axis=1)


def _ep_sigmoid(accs, extra):
    return _sigmoid(accs[0])


def _ep_silu(accs, extra):
    return _silu(accs[0])


def _ep_merge(accs, extra):
    return extra[0] * accs[0] + extra[1] * accs[1]


def _ep_residual(accs, extra):
    return extra[0] + extra[1] * accs[0]


def _ep_swiglu(accs, extra):
    return _silu(accs[0]) * accs[1]


def _proj_conv_body(a_ref, wt_ref, cw_ref, o_ref, halo_ref, *, rows):
    i, j = pl.program_id(0), pl.program_id(1)

    @pl.when(i == 0)
    def _():
        halo_ref[j] = jnp.zeros(halo_ref.shape[1:], F32)

    w = wt_ref[...].astype(BF16)
    cw = cw_ref[...]
    prev = halo_ref[j]
    for r in range(o_ref.shape[0] // rows):
        rs = slice(r * rows, (r + 1) * rows)
        acc = lax.dot_general(a_ref[rs, :], w, (((1,), (1,)), ((), ())), preferred_element_type=F32)
        ext = jnp.concatenate([prev, acc], axis=0)
        back1 = pltpu.roll(ext, 1, axis=0)
        older = pltpu.roll(ext * cw[1:2, :] + back1 * cw[0:1, :], 2, axis=0)
        y = acc * cw[3:4, :] + back1[SUBLANES:, :] * cw[2:3, :] + older[SUBLANES:, :]
        o_ref[rs, :] = _silu(y).astype(o_ref.dtype)
        prev = acc[rows - SUBLANES:, :]
    halo_ref[j] = prev


def _proj_conv(a, w_t, col0, conv_w, *, tm, tn):
    m, k = a.shape
    n = conv_w.shape[1]
    tm = min(tm, m)
    assert col0 % tn == 0 and n % tn == 0
    assert conv_w.shape[0] == GDN_CONV == 4, "the kernel body spells out four taps"
    return pl.pallas_call(
        functools.partial(_proj_conv_body, rows=min(tm, CONV_ROW_CHUNK)),
        grid=(m // tm, n // tn),
        in_specs=[pl.BlockSpec((tm, k), lambda i, j: (i, 0)),
                  pl.BlockSpec((tn, k), lambda i, j, off=col0 // tn: (j + off, 0)),
                  pl.BlockSpec((GDN_CONV, tn), lambda i, j: (0, j))],
        out_specs=pl.BlockSpec((tm, tn), lambda i, j: (i, j)),
        out_shape=jax.ShapeDtypeStruct((m, n), BF16),
        scratch_shapes=[pltpu.VMEM((n // tn, SUBLANES, tn), F32)],
        compiler_params=_params("arbitrary", "arbitrary"),
        name="proj_qkv_b_conv",
    )(a, w_t, conv_w)


def _sb_body(q_ref, k_ref, v_ref, o_ref, acc_ref, carry_ref, *, tq, tk, hb):
    i = pl.program_id(1)
    ratio = tq // tk
    row = lax.broadcasted_iota(jnp.int32, (tq, tk), 0)
    col = lax.broadcasted_iota(jnp.int32, (tq, tk), 1)
    suffix = (lax.broadcasted_iota(jnp.int32, (tk, tk), 0)
              > lax.broadcasted_iota(jnp.int32, (tk, tk), 1)).astype(BF16)
    heads = [slice(s * HEAD_DIM, (s + 1) * HEAD_DIM) for s in range(hb)]

    def tile(j, causal):
        start = pl.multiple_of(j * tk, tk)
        zs = [lax.dot_general(q_ref[:, h], k_ref[pl.ds(start, tk), h], (((1,), (1,)), ((), ())),
                              preferred_element_type=F32) for h in heads]
        lss, l1ms = [], []
        for z in zs:
            sp = jnp.log(1.0 + jnp.exp2(jnp.abs(z) * (-LOG2E)))
            ls = jnp.minimum(z, 0.0) - sp
            l1m = ls - z
            if causal is not None:
                l1m = jnp.where(causal, l1m, 0.0)
            lss.append(ls)
            l1ms.append(l1m)
        withins = [jnp.dot(l1m.astype(BF16), suffix, preferred_element_type=F32) for l1m in l1ms]
        ws = []
        for h, ls, l1m, within in zip(heads, lss, l1ms, withins):
            carry = carry_ref[:, h]
            w = jnp.exp(ls + within + jnp.concatenate([carry] * (tk // LANES), axis=1))
            if causal is not None:
                w = jnp.where(causal, w, 0.0)
            ws.append(w.astype(BF16))
            carry_ref[:, h] = carry + jnp.sum(l1m, axis=1, keepdims=True)
        for h, w in zip(heads, ws):
            acc_ref[:, h] += jnp.dot(w, v_ref[pl.ds(start, tk), h], preferred_element_type=F32)

    acc_ref[...] = jnp.zeros_like(acc_ref)
    carry_ref[...] = jnp.zeros_like(carry_ref)
    for dj in reversed(range(ratio)):
        tile(i * ratio + dj, col + dj * tk < row)

    def more(st):
        n, live = st
        return jnp.logical_and(n < i * ratio, live)

    def body(st):
        n, _ = st
        tile(i * ratio - 1 - n, None)
        return n + 1, jnp.max(carry_ref[...]) > -EXP_UNDERFLOW

    lax.while_loop(more, body, (jnp.int32(0), jnp.bool_(True)))
    o_ref[...] = acc_ref[...].astype(o_ref.dtype)


def _sb_attention(qk, v, *, tq=SB_Q_TILE, tk=SB_K_TILE, hb=SB_HEADS_PER_STEP):
    s = v.shape[0]
    groups = N_HEADS // hb
    wide = hb * HEAD_DIM
    return pl.pallas_call(
        functools.partial(_sb_body, tq=tq, tk=tk, hb=hb),
        grid=(groups, s // tq),
        in_specs=[pl.BlockSpec((tq, wide), lambda g, i: (i, g)),
                  pl.BlockSpec((s, wide), lambda g, i: (0, groups + g)),
                  pl.BlockSpec((s, wide), lambda g, i: (0, g))],
        out_specs=pl.BlockSpec((tq, wide), lambda g, i: (i, g)),
        out_shape=jax.ShapeDtypeStruct((s, N_HEADS * HEAD_DIM), BF16),
        scratch_shapes=[pltpu.VMEM((tq, wide), F32), pltpu.VMEM((tq, wide), F32)],
        compiler_params=_params("parallel", "arbitrary"),
        name="stick_breaking",
    )(qk, qk, v)


def _gdn_body(q_ref, k_ref, v_ref, ba_ref, z_ref, alog_ref, dtb_ref, onw_ref, o_ref,
              state_ref, gct_ref, *, c, hb):
    grp = pl.program_id(0)
    step = pl.program_id(1)
    heads = range(hb)

    @pl.when(step == 0)
    def _():
        state_ref[...] = jnp.zeros_like(state_ref)

    row = lax.broadcasted_iota(jnp.int32, (c, c), 0)
    col = lax.broadcasted_iota(jnp.int32, (c, c), 1)
    lane = lax.broadcasted_iota(jnp.int32, (1, LANES), 1)
    lower = row >= col
    eye = (row == col).astype(F32)

    ba = ba_ref[...]
    beta_all = _sigmoid(ba)
    g_all = -jnp.exp(alog_ref[...]) * _softplus(ba + dtb_ref[...])
    gc_all = jnp.dot(lower.astype(F32), g_all, precision=lax.Precision.HIGHEST,
                     preferred_element_type=F32)
    gct_ref[...] = gc_all.T

    def head(x_ref, s):
        return x_ref[:, s * HEAD_DIM:(s + 1) * HEAD_DIM].astype(F32)

    def l2norm(x):
        return x * lax.rsqrt(jnp.sum(x * x, axis=-1, keepdims=True) + EPS)

    def column(x, idx):
        picked = jnp.sum(jnp.where(lane == idx, x, 0.0), axis=1, keepdims=True)
        return jnp.broadcast_to(picked, (c, LANES))

    qs = [l2norm(head(q_ref, s)) * (HEAD_DIM ** -0.5) for s in heads]
    ks = [l2norm(head(k_ref, s)) for s in heads]
    vs = [head(v_ref, s) for s in heads]
    betas = [column(beta_all, grp * hb + s) for s in heads]
    gcs = [column(gc_all, N_HEADS + grp * hb + s) for s in heads]
    gc_rows = [gct_ref[pl.ds(N_HEADS + grp * hb + s, 1), :] for s in heads]
    decays = [jnp.where(lower, jnp.exp(jnp.where(lower, gc - gc_row, 0.0)), 0.0)
              for gc, gc_row in zip(gcs, gc_rows)]

    kks = [_bdot_nt(k, k) for k in ks]
    qks = [_bdot_nt(q, k) for q, k in zip(qs, ks)]
    attns = [qk * decay for qk, decay in zip(qks, decays)]
    stricts = [jnp.where(row > col, beta * kk * decay, 0.0)
               for beta, kk, decay in zip(betas, kks, decays)]

    invs = [eye - jnp.where((row >> 1) == (col >> 1), st, 0.0) for st in stricts]
    lvl = 1
    while (1 << lvl) < c:
        in_pair = ((row >> (lvl + 1)) == (col >> (lvl + 1))) & ((row >> lvl) != (col >> lvl))
        offs = [jnp.where(in_pair, st, 0.0).astype(BF16) for st in stricts]
        inv16 = [inv.astype(BF16) for inv in invs]
        left = [jnp.dot(a, b, preferred_element_type=F32) for a, b in zip(inv16, offs)]
        corr = [jnp.dot(a.astype(BF16), b, preferred_element_type=F32) for a, b in zip(left, inv16)]
        invs = [inv - cr for inv, cr in zip(invs, corr)]
        lvl += 1

    exp_gcs = [jnp.exp(gc) for gc in gcs]
    rhss = [jnp.concatenate([beta * v, beta * eg * k], axis=1)
            for beta, v, eg, k in zip(betas, vs, exp_gcs, ks)]
    w_vks = [_bdot(inv, rhs) for inv, rhs in zip(invs, rhss)]

    states = [state_ref[s] for s in heads]
    us = [w_vk[:, :HEAD_DIM] - _bdot(w_vk[:, HEAD_DIM:], st) for w_vk, st in zip(w_vks, states)]
    inter = [_bdot(q * eg, st) for q, eg, st in zip(qs, exp_gcs, states)]
    intra = [_bdot(attn, u) for attn, u in zip(attns, us)]
    gc_lasts = [gc[c - 1:c, :] for gc in gcs]
    k_decs = [k * jnp.exp(gl - gc) for k, gl, gc in zip(ks, gc_lasts, gcs)]
    upd = [_bdot(kd.T, u) for kd, u in zip(k_decs, us)]
    for s in heads:
        state_ref[s] = jnp.exp(gc_lasts[s]) * states[s] + upd[s]
        o = inter[s] + intra[s]
        ms = jnp.mean(o * o, axis=-1, keepdims=True)
        o_ref[:, s * HEAD_DIM:(s + 1) * HEAD_DIM] = (
            o * lax.rsqrt(ms + EPS) * onw_ref[...] * head(z_ref, s)).astype(o_ref.dtype)


def _gdn(qkv, ba, z_gate, alog_row, dtb_row, onw_row, *, c=GDN_CHUNK, hb=GDN_HEADS_PER_STEP):
    s = qkv.shape[0]
    groups = N_HEADS // hb
    wide = hb * HEAD_DIM
    sec = lambda k: pl.BlockSpec((c, wide), lambda g, t, k=k: (t, k * groups + g))
    row = pl.BlockSpec((1, LANES), lambda g, t: (0, 0))
    return pl.pallas_call(
        functools.partial(_gdn_body, c=c, hb=hb),
        grid=(groups, s // c),
        in_specs=[sec(0), sec(1), sec(2),
                  pl.BlockSpec((c, LANES), lambda g, t: (t, 0)),
                  pl.BlockSpec((c, wide), lambda g, t: (t, g)),
                  row, row, row],
        out_specs=pl.BlockSpec((c, wide), lambda g, t: (t, g)),
        out_shape=jax.ShapeDtypeStruct((s, N_HEADS * HEAD_DIM), BF16),
        scratch_shapes=[pltpu.VMEM((hb, HEAD_DIM, HEAD_DIM), F32),
                        pltpu.VMEM((LANES, c), F32)],
        compiler_params=_params("parallel", "arbitrary"),
        name="gated_deltanet",
    )(qkv, qkv, qkv, ba, z_gate, alog_row, dtb_row, onw_row)


def kernel(x, c, w_mod, b_mod, norm1_w, w_in, q_norm_w, k_norm_w, conv_w, a_log, dt_bias,
           o_norm_w, p_a, p_b, w_out, norm2_w, w_gate, w_up, w_down):
    bsz, s, d = x.shape
    depth = w_mod.shape[0]
    assert bsz == 1, "single-sequence prefill block"
    hd = N_HEADS * HEAD_DIM
    d_ff = w_gate.shape[2]
    h = x[0].astype(F32)
    c_col = c.astype(F32).reshape(d, 1)

    o_v, o_qkvb, o_z = 2 * hd, 3 * hd, 6 * hd
    o_b, o_gates = 7 * hd, 7 * hd + 2 * N_HEADS

    for l in range(depth):
        mod = _adaln(c_col, w_mod[l].astype(F32), b_mod[l].astype(F32).reshape(1, 6 * d))
        shift1, scale1, gate1, shift2, scale2, gate2 = [mod[:, i * d:(i + 1) * d] for i in range(6)]

        u = _norm_mod(h, norm1_w[l].astype(F32).reshape(1, d), scale1, shift1)

        w_t = jnp.transpose(w_in[l])
        tm, tn = TILES["in_proj"]
        qk_norm_w = jnp.concatenate([jnp.tile(q_norm_w[l].astype(F32) * (HEAD_DIM ** -0.5), N_HEADS),
                                     jnp.tile(k_norm_w[l].astype(F32), N_HEADS)]).reshape(1, 2 * hd)
        qk_a = _matmul([(u, w_t, 0, "nk")], [(qk_norm_w, "row", 0)], _ep_head_rmsnorm, 2 * hd, BF16,
                       tm=tm, tn=tn, name="proj_qk")
        v_a = _matmul([(u, w_t, o_v, "nk")], [], _ep_plain, hd, BF16, tm=tm, tn=tn, name="proj_v")
        qkv_b = _proj_conv(u, w_t, o_qkvb, conv_w[l].astype(F32), tm=tm, tn=tn)
        z_gate = _matmul([(u, w_t, o_z, "nk")], [], _ep_silu, hd, BF16, tm=tm, tn=tn, name="proj_z")
        ba = _matmul([(u, w_t, o_b, "nk")], [], _ep_plain, LANES, F32, tm=tm, tn=LANES, name="proj_ba")
        sig_gates = _matmul([(u, w_t, o_gates, "nk")], [], _ep_sigmoid, 2 * d, BF16, tm=tm, tn=tn,
                            name="proj_gates")

        o_a = _sb_attention(qk_a, v_a)

        pad = (0, LANES - 2 * N_HEADS)
        alog_row = jnp.pad(jnp.concatenate([jnp.zeros((N_HEADS,), F32), a_log[l].astype(F32)]), pad)
        dtb_row = jnp.pad(jnp.concatenate([jnp.zeros((N_HEADS,), F32), dt_bias[l].astype(F32)]), pad)
        o_b_out = _gdn(qkv_b, ba, z_gate, alog_row.reshape(1, LANES), dtb_row.reshape(1, LANES),
                       o_norm_w[l].astype(F32).reshape(1, HEAD_DIM))

        tm, tn = TILES["merge"]
        merged = _matmul([(o_a, p_a[l], 0, "kn"), (o_b_out, p_b[l], 0, "kn")],
                         [(sig_gates, "tile", 0), (sig_gates, "tile", d // tn)], _ep_merge, d, BF16,
                         tm=tm, tn=tn, name="merge_proj")
        tm, tn = TILES["out_proj"]
        h = _matmul([(merged, w_out[l], 0, "kn")], [(h, "tile", 0), (gate1, "row", 0)],
                    _ep_residual, d, F32, tm=tm, tn=tn, name="out_proj")

        u2 = _norm_mod(h, norm2_w[l].astype(F32).reshape(1, d), scale2, shift2)
        tm, tn = TILES["ffn_up"]
        ff = _matmul([(u2, w_gate[l], 0, "kn"), (u2, w_up[l], 0, "kn")], [], _ep_swiglu, d_ff, BF16,
                     tm=tm, tn=tn, name="ffn_up")
        tm, tn = TILES["ffn_down"]
        h = _matmul([(ff, w_down[l].astype(BF16), 0, "kn")], [(h, "tile", 0), (gate2, "row", 0)],
                    _ep_residual, d, F32, tm=tm, tn=tn, name="ffn_down")
    return h.reshape(bsz, s, d).astype(x.dtype)
```
